```python
import jax
import jax.numpy as jnp
from jax import lax
import numpy as np

D_MODEL = 4096
BATCH = 2
SEQ = 8192
DEPTH = 4

D_MIX = D_MODEL
MLSTM_WIDTH = D_MIX // 2
N_MLSTM_HEADS = 4
MLSTM_DV = MLSTM_WIDTH // N_MLSTM_HEADS
MLSTM_DK = MLSTM_DV // 2
MLSTM_QK_WIDTH = N_MLSTM_HEADS * MLSTM_DK
N_GATE_COLS = 4 * N_MLSTM_HEADS
MLSTM_CHUNK = 64
CONV_W = 5
ATTN_WIDTH = D_MIX - MLSTM_WIDTH
ATTN_HEAD_DIM = 128
N_ATTN_HEADS = ATTN_WIDTH // ATTN_HEAD_DIM
DILATED_PATTERNS = ((128, 1), (512, 4), (2048, 16))
N_EXPERT_GROUPS = 4
EXPERTS_PER_GROUP = 4
N_EXPERTS = N_EXPERT_GROUPS * EXPERTS_PER_GROUP
EXPERT_TOPK = 2
EXPERT_FF = 768
EXPERT_BLOCK = 256
SEG_SIZES = (2 * MLSTM_QK_WIDTH, MLSTM_WIDTH, MLSTM_WIDTH, N_GATE_COLS, ATTN_WIDTH, ATTN_WIDTH, ATTN_WIDTH)
C_IN = sum(SEG_SIZES)
RMS_EPS = 1e-6
MASK_VALUE = -1e30

kernel_name = 'hymba_mlstm_dilated_hmoe_encoder'


def rmsnorm(x, g):
    xf = x.astype(jnp.float32)
    y = xf * lax.rsqrt(jnp.mean(xf * xf, axis=-1, keepdims=True) + RMS_EPS)
    return (y * g.astype(jnp.float32)).astype(x.dtype)


def head_rmsnorm(t, g):
    B, H, S, d = t.shape
    tf = t.astype(jnp.float32)
    tf = tf * lax.rsqrt(jnp.mean(tf * tf, axis=-1, keepdims=True) + RMS_EPS)
    tf = tf.transpose(0, 2, 1, 3).reshape(B, S, H * d)
    return (tf * g.astype(jnp.float32)).astype(t.dtype)


def split_heads(t, n_heads):
    B, S, W = t.shape
    return t.reshape(B, S, n_heads, W // n_heads).transpose(0, 2, 1, 3)


def short_conv(u, w, b):
    C = u.shape[-1]
    y = lax.conv_general_dilated(u, w[:, None, :].astype(u.dtype), window_strides=(1,),
                                 padding=[(CONV_W // 2, CONV_W // 2)],
                                 dimension_numbers=('NWC', 'WIO', 'NWC'),
                                 feature_group_count=C)
    return y + b.astype(u.dtype)


def mlstm_chunkwise(q, k, v, i_pre, log_f):
    B, H, S, DK = q.shape
    DV = v.shape[-1]
    n_chunks = S // MLSTM_CHUNK

    def to_chunks(t):
        t = t.reshape((B, H, n_chunks, MLSTM_CHUNK) + t.shape[3:])
        return jnp.moveaxis(t, 2, 0)

    lower = jnp.tril(jnp.ones((MLSTM_CHUNK, MLSTM_CHUNK), dtype=bool))

    def step(carry, inp):
        c_state, n_state, m_state = carry
        qc, kc, vc, ic, fc = inp
        b = jnp.cumsum(fc, axis=-1)
        d_log = jnp.where(lower, b[..., :, None] - b[..., None, :] + ic[..., None, :], -jnp.inf)
        a_inter = b + m_state[..., None]
        m_t = jnp.maximum(a_inter, jnp.max(d_log, axis=-1))
        w_intra = jnp.exp(d_log - m_t[..., None])
        w_inter = jnp.exp(a_inter - m_t)
        s_qk = jnp.einsum('bhtk,bhsk->bhts', qc, kc) * w_intra
        num = (jnp.einsum('bhts,bhsv->bhtv', s_qk, vc)
               + w_inter[..., None] * jnp.einsum('bhtk,bhvk->bhtv', qc, c_state))
        den = jnp.sum(s_qk, axis=-1) + w_inter * jnp.einsum('bhtk,bhk->bht', qc, n_state)
        h = num / jnp.maximum(jnp.abs(den), jnp.exp(-m_t))[..., None]
        b_last = b[..., -1]
        g = b_last[..., None] - b + ic
        m_new = jnp.maximum(b_last + m_state, jnp.max(g, axis=-1))
        decay = jnp.exp(b_last + m_state - m_new)
        w_add = jnp.exp(g - m_new[..., None])
        c_state = decay[..., None, None] * c_state + jnp.einsum('bhsv,bhsk->bhvk', vc * w_add[..., None], kc)
        n_state = decay[..., None] * n_state + jnp.einsum('bhs,bhsk->bhk', w_add, kc)
        return (c_state, n_state, m_new), h

    init = (jnp.zeros((B, H, DV, DK), jnp.float32), jnp.zeros((B, H, DK), jnp.float32),
            jnp.zeros((B, H), jnp.float32))
    xs = (to_chunks(q), to_chunks(k), to_chunks(v), to_chunks(i_pre), to_chunks(log_f))
    _, h = lax.scan(step, init, xs)
    return jnp.moveaxis(h, 0, 2).reshape(B, H, S, DV)


def mlstm_bidirectional(q, k, v, gates):
    B, S, _ = gates.shape
    g = gates.reshape(B, S, 4, N_MLSTM_HEADS).transpose(2, 0, 3, 1)
    qf = q.astype(jnp.float32) * (MLSTM_DK ** -0.5)
    kf = k.astype(jnp.float32)
    vf = v.astype(jnp.float32)
    fwd = mlstm_chunkwise(qf, kf, vf, g[0], jax.nn.log_sigmoid(g[1]))
    flip = lambda t: jnp.flip(t, axis=2)
    bwd = flip(mlstm_chunkwise(flip(qf), flip(kf), flip(vf), flip(g[2]), flip(jax.nn.log_sigmoid(g[3]))))
    return (fwd + bwd).astype(v.dtype)


def dilated_band_attention(q, k, v, slopes, dilation, half):
    B, H, S, Dh = q.shape
    L = S // dilation
    nb = -(-L // half)
    Lp = nb * half

    def to_blocks(t):
        t = t.reshape(B, H, L, dilation, Dh).transpose(0, 1, 3, 2, 4)
        t = jnp.pad(t, ((0, 0), (0, 0), (0, 0), (0, Lp - L), (0, 0)))
        return t.reshape(B, H, dilation, nb, half, Dh)

    def band(t):
        tp = jnp.pad(t, ((0, 0), (0, 0), (0, 0), (1, 1), (0, 0), (0, 0)))
        return jnp.concatenate([tp[:, :, :, 0:nb], tp[:, :, :, 1:nb + 1], tp[:, :, :, 2:nb + 2]], axis=4)

    qb = to_blocks(q)
    kb = band(to_blocks(k))
    vb = band(to_blocks(v))
    s = jnp.einsum('bhrnqe,bhrnke->bhrnqk', qb, kb, preferred_element_type=jnp.float32)
    qi = jnp.arange(half)
    kj = jnp.arange(3 * half)
    rel = kj[None, :] - half - qi[:, None]
    key_pos = (jnp.arange(nb)[:, None] - 1) * half + kj[None, :]
    valid = ((jnp.abs(rel) <= half)[None]
             & ((key_pos >= 0) & (key_pos < L))[:, None, :])
    dist = (dilation * jnp.abs(rel)).astype(jnp.float32)
    alibi = -slopes[:, None, None] * dist[None]
    s = jnp.where(valid[None, None, None], s + alibi[None, :, None, None], MASK_VALUE)
    lse = jax.nn.logsumexp(s, axis=-1)
    p = jnp.exp(s - lse[..., None])
    o = jnp.einsum('bhrnqk,bhrnke->bhrnqe', p.astype(v.dtype), vb)
    o = o.reshape(B, H, dilation, Lp, Dh)[:, :, :, :L].transpose(0, 1, 3, 2, 4).reshape(B, H, S, Dh)
    lse = lse.reshape(B, H, dilation, Lp)[:, :, :, :L].transpose(0, 1, 3, 2).reshape(B, H, S)
    return o, lse


def dilated_mixture_attention(q, k, v):
    H = q.shape[1]
    slopes = jnp.exp2(-8.0 * jnp.arange(1, H + 1, dtype=jnp.float32) / H)
    qs = q * (ATTN_HEAD_DIM ** -0.5)
    outs = []
    lses = []
    for window, dil in DILATED_PATTERNS:
        o, l = dilated_band_attention(qs, k, v, slopes, dil, window // (2 * dil))
        outs.append(o.astype(jnp.float32))
        lses.append(l)
    wts = jax.nn.softmax(jnp.stack(lses, axis=0), axis=0)
    out = jnp.sum(wts[..., None] * jnp.stack(outs, axis=0), axis=0)
    return out.astype(q.dtype)


def hybrid_mixer(h, w_in, b_gate, conv_w, conv_b, head_g, w_out):
    z = h @ w_in
    qk_m, v_m, o_m, gates, q_a, k_a, v_a = jnp.split(z, np.cumsum(SEG_SIZES)[:-1], axis=-1)
    qk_m = jax.nn.silu(short_conv(qk_m, conv_w, conv_b))
    q_m = split_heads(qk_m[..., :MLSTM_QK_WIDTH], N_MLSTM_HEADS)
    k_m = split_heads(qk_m[..., MLSTM_QK_WIDTH:], N_MLSTM_HEADS)
    gates = gates.astype(jnp.float32) + b_gate.astype(jnp.float32)
    h_m = mlstm_bidirectional(q_m, k_m, split_heads(v_m, N_MLSTM_HEADS), gates)
    y_m = head_rmsnorm(h_m, head_g[:MLSTM_WIDTH]) * jax.nn.sigmoid(o_m)
    o_att = dilated_mixture_attention(split_heads(q_a, N_ATTN_HEADS), split_heads(k_a, N_ATTN_HEADS),
                                      split_heads(v_a, N_ATTN_HEADS))
    y_a = head_rmsnorm(o_att, head_g[MLSTM_WIDTH:])
    return jnp.concatenate([y_m, y_a], axis=-1) @ w_out


def hierarchical_moe(h, w_rg, b_rg, w_re, b_re, w_gate, w_up, w_down):
    Bsz, S, D = h.shape
    T = Bsz * S
    xt = h.reshape(T, D)
    g_prob = jax.nn.softmax((xt @ w_rg + b_rg).astype(jnp.float32), axis=-1)
    g_top_p, g_top = lax.top_k(g_prob, 1)
    e_logits = (xt @ w_re + b_re).astype(jnp.float32).reshape(T, N_EXPERT_GROUPS, EXPERTS_PER_GROUP)
    e_sel = e_logits[jnp.arange(T), g_top[:, 0]]
    e_top_p, e_top = lax.top_k(jax.nn.softmax(e_sel, axis=-1), EXPERT_TOPK)
    w_tok = g_top_p * e_top_p / jnp.sum(e_top_p, axis=-1, keepdims=True)
    expert_id = g_top * EXPERTS_PER_GROUP + e_top
    A = T * EXPERT_TOPK
    flat_e = expert_id.reshape(A).astype(jnp.int32)
    flat_w = w_tok.reshape(A)
    flat_tok = jnp.repeat(jnp.arange(T, dtype=jnp.int32), EXPERT_TOPK)
    order = jnp.argsort(flat_e)
    sorted_e = flat_e[order]
    counts = jnp.bincount(flat_e, length=N_EXPERTS)
    padded = ((counts + EXPERT_BLOCK - 1) // EXPERT_BLOCK) * EXPERT_BLOCK
    pad_end = jnp.cumsum(padded)
    pad_start = pad_end - padded
    start = jnp.cumsum(counts) - counts
    dest = pad_start[sorted_e] + jnp.arange(A) - start[sorted_e]
    n_blocks = -(-A // EXPERT_BLOCK) + N_EXPERTS
    P = n_blocks * EXPERT_BLOCK
    slot_tok = jnp.zeros((P,), jnp.int32).at[dest].set(flat_tok[order])
    slot_w = jnp.zeros((P,), jnp.float32).at[dest].set(flat_w[order])
    block_e = jnp.clip(jnp.searchsorted(pad_end, jnp.arange(n_blocks) * EXPERT_BLOCK, side='right'),
                       0, N_EXPERTS - 1)
    xb = xt[slot_tok].reshape(n_blocks, EXPERT_BLOCK, D)

    def expert_block(args):
        xblk, e = args
        return (jax.nn.silu(xblk @ w_gate[e]) * (xblk @ w_up[e])) @ w_down[e]

    yb = lax.map(expert_block, (xb, block_e)).reshape(P, D)
    contrib = (yb.astype(jnp.float32) * slot_w[:, None]).astype(xt.dtype)
    y = jnp.zeros((T, D), xt.dtype).at[slot_tok].add(contrib)
    return y.reshape(Bsz, S, D)


def setup_inputs(seed: int = 0) -> dict:
    key = jax.random.key(seed)
    ks = jax.random.split(key, 20)
    f32 = jnp.float32

    def nrm(k, shape, scale):
        return jax.random.normal(k, shape, f32) * scale

    x = nrm(ks[0], (BATCH, SEQ, D_MODEL), 1.0)
    norm1_g = 1.0 + nrm(ks[1], (DEPTH, D_MODEL), 0.02)
    w_in = nrm(ks[2], (DEPTH, D_MODEL, C_IN), D_MODEL ** -0.5)
    i_bias = nrm(ks[3], (DEPTH, 2, N_MLSTM_HEADS), 0.1)
    f_bias = jnp.linspace(3.0, 6.0, N_MLSTM_HEADS, dtype=f32) + nrm(ks[4], (DEPTH, 2, N_MLSTM_HEADS), 0.1)
    b_gate = jnp.stack([i_bias[:, 0], f_bias[:, 0], i_bias[:, 1], f_bias[:, 1]], axis=1).reshape(DEPTH, N_GATE_COLS)
    conv_w = nrm(ks[5], (DEPTH, CONV_W, 2 * MLSTM_QK_WIDTH), CONV_W ** -0.5)
    conv_b = nrm(ks[6], (DEPTH, 2 * MLSTM_QK_WIDTH), 0.01)
    head_norm_g = 1.0 + nrm(ks[7], (DEPTH, D_MIX), 0.02)
    w_out = nrm(ks[8], (DEPTH, D_MIX, D_MODEL), D_MIX ** -0.5)
    norm2_g = 1.0 + nrm(ks[9], (DEPTH, D_MODEL), 0.02)
    w_router_group = nrm(ks[10], (DEPTH, D_MODEL, N_EXPERT_GROUPS), D_MODEL ** -0.5)
    b_router_group = nrm(ks[11], (DEPTH, N_EXPERT_GROUPS), 0.01)
    w_router_expert = nrm(ks[12], (DEPTH, D_MODEL, N_EXPERTS), D_MODEL ** -0.5)
    b_router_expert = nrm(ks[13], (DEPTH, N_EXPERTS), 0.01)
    w_gate = nrm(ks[14], (DEPTH, N_EXPERTS, D_MODEL, EXPERT_FF), D_MODEL ** -0.5)
    w_up = nrm(ks[15], (DEPTH, N_EXPERTS, D_MODEL, EXPERT_FF), D_MODEL ** -0.5)
    w_down = nrm(ks[16], (DEPTH, N_EXPERTS, EXPERT_FF, D_MODEL), EXPERT_FF ** -0.5)
    final_norm_g = 1.0 + nrm(ks[17], (D_MODEL,), 0.02)
    return {'x': x, 'norm1_g': norm1_g, 'w_in': w_in, 'b_gate': b_gate, 'conv_w': conv_w,
            'conv_b': conv_b, 'head_norm_g': head_norm_g, 'w_out': w_out, 'norm2_g': norm2_g,
            'w_router_group': w_router_group, 'b_router_group': b_router_group,
            'w_router_expert': w_router_expert, 'b_router_expert': b_router_expert,
            'w_gate': w_gate, 'w_up': w_up, 'w_down': w_down, 'final_norm_g': final_norm_g}


def reference(x, norm1_g, w_in, b_gate, conv_w, conv_b, head_norm_g, w_out, norm2_g,
              w_router_group, b_router_group, w_router_expert, b_router_expert,
              w_gate, w_up, w_down, final_norm_g):
    for l in range(DEPTH):
        h = rmsnorm(x, norm1_g[l])
        x = x + hybrid_mixer(h, w_in[l], b_gate[l], conv_w[l], conv_b[l], head_norm_g[l], w_out[l])
        h = rmsnorm(x, norm2_g[l])
        x = x + hierarchical_moe(h, w_router_group[l], b_router_group[l], w_router_expert[l],
                                 b_router_expert[l], w_gate[l], w_up[l], w_down[l])
    return rmsnorm(x, final_norm_g)
```

```python
import functools

import jax
import jax.numpy as jnp
from jax import lax
from jax.experimental import pallas as pl
from jax.experimental.pallas import tpu as pltpu

MLSTM_DV = 512
MLSTM_DK = 256
ATTN_HEAD_DIM = 128
CONV_W = 5
DILATED_PATTERNS = ((128, 1), (512, 4), (2048, 16))
ATTN_HALF = 64
EXPERT_TOPK = 2
RMS_EPS = 1e-6
MASK_VALUE = -1e30

LANES = 128
SUBLANES = 8
VMEM_LIMIT_BYTES = 56 * 1024 * 1024

MLSTM_CHUNK = 256
EXPERT_BLOCK = 256
ROUTER_LANES = 128

BF16 = jnp.bfloat16
F32 = jnp.float32


def _params(semantics):
    return pltpu.CompilerParams(dimension_semantics=semantics, vmem_limit_bytes=VMEM_LIMIT_BYTES)


def _rmsnorm_kernel(x_ref, g_ref, o_ref):
    x = x_ref[...]
    ms = jnp.mean(x * x, axis=-1, keepdims=True)
    o_ref[...] = (x * lax.rsqrt(ms + RMS_EPS) * g_ref[...]).astype(o_ref.dtype)


def _rmsnorm(x2d, g, out_dtype, tm=256):
    t, d = x2d.shape
    return pl.pallas_call(
        _rmsnorm_kernel,
        grid=(t // tm,),
        in_specs=[pl.BlockSpec((tm, d), lambda i: (i, 0)),
                  pl.BlockSpec((1, d), lambda i: (0, 0))],
        out_specs=pl.BlockSpec((tm, d), lambda i: (i, 0)),
        out_shape=jax.ShapeDtypeStruct((t, d), out_dtype),
        compiler_params=_params(("parallel",)),
        name="rmsnorm",
    )(x2d, g.reshape(1, d))


def _rmsnorm_router_kernel(x_ref, g_ref, wr_ref, br_ref, h_ref, lg_ref):
    x = x_ref[...]
    ms = jnp.mean(x * x, axis=-1, keepdims=True)
    h = x * lax.rsqrt(ms + RMS_EPS) * g_ref[...]
    h_ref[...] = h.astype(h_ref.dtype)
    lg_ref[...] = jnp.dot(h, wr_ref[...], precision=lax.Precision.HIGHEST,
                          preferred_element_type=F32) + br_ref[...]


def _rmsnorm_router(x2d, g, w_router, b_router, tm=256):
    t, d = x2d.shape
    return pl.pallas_call(
        _rmsnorm_router_kernel,
        grid=(t // tm,),
        in_specs=[pl.BlockSpec((tm, d), lambda i: (i, 0)),
                  pl.BlockSpec((1, d), lambda i: (0, 0)),
                  pl.BlockSpec((d, ROUTER_LANES), lambda i: (0, 0)),
                  pl.BlockSpec((1, ROUTER_LANES), lambda i: (0, 0))],
        out_specs=[pl.BlockSpec((tm, d), lambda i: (i, 0)),
                   pl.BlockSpec((tm, ROUTER_LANES), lambda i: (i, 0))],
        out_shape=[jax.ShapeDtypeStruct((t, d), BF16),
                   jax.ShapeDtypeStruct((t, ROUTER_LANES), F32)],
        compiler_params=_params(("parallel",)),
        name="rmsnorm_router",
    )(x2d, g.reshape(1, d), w_router, b_router)


def _matmul_kernel(a_ref, b_ref, o_ref):
    o_ref[...] = jnp.dot(a_ref[...], b_ref[...], preferred_element_type=F32).astype(o_ref.dtype)


def _matmul(a, b, out_dtype, tm=512, tn=1024):
    m, k = a.shape
    n = b.shape[1]
    tn = min(tn, n)
    tm = min(tm, m)
    return pl.pallas_call(
        _matmul_kernel,
        grid=(n // tn, m // tm),
        in_specs=[pl.BlockSpec((tm, k), lambda j, i: (i, 0)),
                  pl.BlockSpec((k, tn), lambda j, i: (0, j))],
        out_specs=pl.BlockSpec((tm, tn), lambda j, i: (i, j)),
        out_shape=jax.ShapeDtypeStruct((m, n), out_dtype),
        compiler_params=_params(("parallel", "parallel")),
        name="proj_in",
    )(a, b)


def _out_proj_kernel(a1_ref, a2_ref, b1_ref, b2_ref, r_ref, o_ref):
    acc = jnp.dot(a1_ref[...], b1_ref[...], preferred_element_type=F32)
    acc = acc + jnp.dot(a2_ref[...], b2_ref[...], preferred_element_type=F32)
    o_ref[...] = r_ref[...] + acc


def _out_proj(a1, a2, b1, b2, res, tm=512, tn=1024):
    m, k1 = a1.shape
    k2 = a2.shape[1]
    n = b1.shape[1]
    tn = min(tn, n)
    tm = min(tm, m)
    return pl.pallas_call(
        _out_proj_kernel,
        grid=(n // tn, m // tm),
        in_specs=[pl.BlockSpec((tm, k1), lambda j, i: (i, 0)),
                  pl.BlockSpec((tm, k2), lambda j, i: (i, 0)),
                  pl.BlockSpec((k1, tn), lambda j, i: (0, j)),
                  pl.BlockSpec((k2, tn), lambda j, i: (0, j)),
                  pl.BlockSpec((tm, tn), lambda j, i: (i, j))],
        out_specs=pl.BlockSpec((tm, tn), lambda j, i: (i, j)),
        out_shape=jax.ShapeDtypeStruct((m, n), F32),
        compiler_params=_params(("parallel", "parallel")),
        name="proj_out",
    )(a1, a2, b1, b2, res)


def _conv_kernel(up_ref, uc_ref, un_ref, w_ref, b_ref, o_ref, buf_ref, *, ts, scale, transpose):
    t = pl.program_id(2)
    nt = pl.num_programs(2)
    pad = SUBLANES
    buf_ref[0:pad, :] = jnp.where(t > 0, up_ref[0], 0.0)
    buf_ref[pad:pad + ts, :] = uc_ref[0]
    buf_ref[pad + ts:2 * pad + ts, :] = jnp.where(t < nt - 1, un_ref[0], 0.0)
    acc = jnp.zeros((ts, buf_ref.shape[1]), F32) + b_ref[...]
    for j in range(CONV_W):
        acc = acc + buf_ref[pl.ds(pad - CONV_W // 2 + j, ts), :] * w_ref[j:j + 1, :]
    y = acc * jax.nn.sigmoid(acc) * scale
    if transpose:
        o_ref[0] = y.T.astype(o_ref.dtype)
    else:
        o_ref[0] = y.astype(o_ref.dtype)


def _short_conv_silu(u, conv_w, conv_b, col0, width, scale, transpose, ts=512):
    bsz, s, _ = u.shape
    tc = MLSTM_DK
    ts = min(ts, s)
    cb0 = col0 // tc
    nrb = ts // SUBLANES
    last_rb = s // SUBLANES - 1
    kern = functools.partial(_conv_kernel, ts=ts, scale=scale, transpose=transpose)
    if transpose:
        out_spec = pl.BlockSpec((1, tc, ts), lambda b, c, t: (b, c, t))
        out_shape = jax.ShapeDtypeStruct((bsz, width, s), BF16)
    else:
        out_spec = pl.BlockSpec((1, ts, tc), lambda b, c, t: (b, t, c))
        out_shape = jax.ShapeDtypeStruct((bsz, s, width), BF16)
    return pl.pallas_call(
        kern,
        grid=(bsz, width // tc, s // ts),
        in_specs=[pl.BlockSpec((1, SUBLANES, tc),
                               lambda b, c, t: (b, jnp.maximum(t * nrb - 1, 0), cb0 + c)),
                  pl.BlockSpec((1, ts, tc), lambda b, c, t: (b, t, cb0 + c)),
                  pl.BlockSpec((1, SUBLANES, tc),
                               lambda b, c, t: (b, jnp.minimum((t + 1) * nrb, last_rb), cb0 + c)),
                  pl.BlockSpec((CONV_W, tc), lambda b, c, t: (0, cb0 + c)),
                  pl.BlockSpec((1, tc), lambda b, c, t: (0, cb0 + c))],
        out_specs=out_spec,
        out_shape=out_shape,
        scratch_shapes=[pltpu.VMEM((ts + 2 * SUBLANES, tc), F32)],
        compiler_params=_params(("parallel", "parallel", "parallel")),
        name="conv_silu_t" if transpose else "conv_silu",
    )(u, u, u, conv_w, conv_b.reshape(1, -1))


def _log_sigmoid(x):
    return jnp.minimum(x, 0.0) - jnp.log1p(jnp.exp(-jnp.abs(x)))


def _mlstm_kernel(bias_ref, q_ref, kt_ref, v_ref, g_ref, *rest, chunk, nheads, reverse, final):
    if final:
        hprev_ref, og_ref, hg_ref, o_ref, ct_ref, m_ref, vext_ref = rest
    else:
        o_ref, ct_ref, m_ref, vext_ref = rest
    h_idx = pl.program_id(1)
    c_idx = pl.program_id(2)
    dv = MLSTM_DV
    ln = chunk

    @pl.when(c_idx == 0)
    def _():
        ct_ref[...] = jnp.zeros_like(ct_ref)
        m_ref[...] = jnp.zeros_like(m_ref)

    gi, gf = (2, 3) if reverse else (0, 1)
    i_row = g_ref[gi] + bias_ref[gi * nheads + h_idx]
    f_row = _log_sigmoid(g_ref[gf] + bias_ref[gf * nheads + h_idx])

    row = lax.broadcasted_iota(jnp.int32, (ln, ln), 0)
    col = lax.broadcasted_iota(jnp.int32, (ln, ln), 1)
    cum_mask = (row >= col) if reverse else (row <= col)
    b_row = jnp.dot(jnp.broadcast_to(f_row, (SUBLANES, ln)), cum_mask.astype(F32),
                    precision=lax.Precision.HIGHEST, preferred_element_type=F32)[0:1, :]
    total = jnp.sum(f_row, axis=1, keepdims=True)
    b_col = jnp.sum(jnp.where(row == col, b_row, 0.0), axis=1, keepdims=True)

    m0 = m_ref[0:1, 0:1]
    tri = (col >= row) if reverse else (col <= row)
    d_log = jnp.where(tri, b_col - b_row + i_row, MASK_VALUE)
    a_col = b_col + m0
    m_t = jnp.maximum(a_col, jnp.max(d_log, axis=1, keepdims=True))
    w_intra = jnp.exp(d_log - m_t)
    w_inter = jnp.exp(a_col - m_t)

    q = q_ref[0]
    kt = kt_ref[0]
    vext_ref[:, 0:dv] = v_ref[0]
    vext_ref[:, dv:] = jnp.ones((ln, LANES), vext_ref.dtype)
    vext = vext_ref[...]

    s_qk = jnp.dot(q, kt, preferred_element_type=F32) * w_intra
    intra = jnp.dot(s_qk.astype(BF16), vext, preferred_element_type=F32)
    ct = ct_ref[...]
    inter = jnp.dot(q, ct.astype(BF16), preferred_element_type=F32)
    tot = intra + w_inter * inter
    den = tot[:, dv:dv + 1]
    h = tot[:, 0:dv] / jnp.maximum(jnp.abs(den), jnp.exp(-m_t))

    g_row = total - b_row + i_row
    m_new = jnp.maximum(total + m0, jnp.max(g_row, axis=1, keepdims=True))
    decay = jnp.exp(total + m0 - m_new)
    w_add = jnp.exp(g_row - m_new)
    ktw = (kt.astype(F32) * w_add).astype(BF16)
    ct_ref[...] = decay * ct + jnp.dot(ktw, vext, preferred_element_type=F32)
    m_ref[...] = jnp.broadcast_to(m_new, m_ref.shape)

    if final:
        hs = h + hprev_ref[0]
        hn = hs * lax.rsqrt(jnp.mean(hs * hs, axis=-1, keepdims=True) + RMS_EPS)
        o_ref[0] = (hn * hg_ref[...] * jax.nn.sigmoid(og_ref[0])).astype(o_ref.dtype)
    else:
        o_ref[0] = h


def _mlstm_scan(bias, q, kt, v, gates_t, reverse, hprev=None, ogate=None, head_g=None):
    bsz, s, _ = q.shape
    nheads = v.shape[2] // MLSTM_DV
    ln = min(MLSTM_CHUNK, s)
    nc = s // ln
    final = hprev is not None

    def cidx(c):
        return (nc - 1 - c) if reverse else c

    in_specs = [pl.BlockSpec(memory_space=pltpu.SMEM),
                pl.BlockSpec((1, ln, MLSTM_DK), lambda b, h, c: (b, cidx(c), h)),
                pl.BlockSpec((1, MLSTM_DK, ln), lambda b, h, c: (b, h, cidx(c))),
                pl.BlockSpec((1, ln, MLSTM_DV), lambda b, h, c: (b, cidx(c), h)),
                pl.BlockSpec((None, 4, None, 1, ln), lambda b, h, c: (b, 0, h, 0, cidx(c)))]
    args = [bias, q, kt, v, gates_t]
    if final:
        in_specs += [pl.BlockSpec((1, ln, MLSTM_DV), lambda b, h, c: (b, cidx(c), h)),
                     pl.BlockSpec((1, ln, MLSTM_DV), lambda b, h, c: (b, cidx(c), h)),
                     pl.BlockSpec((1, MLSTM_DV), lambda b, h, c: (0, h))]
        args += [hprev, ogate, head_g]
    kern = functools.partial(_mlstm_kernel, chunk=ln, nheads=nheads, reverse=reverse, final=final)
    return pl.pallas_call(
        kern,
        grid=(bsz, nheads, nc),
        in_specs=in_specs,
        out_specs=pl.BlockSpec((1, ln, MLSTM_DV), lambda b, h, c: (b, cidx(c), h)),
        out_shape=jax.ShapeDtypeStruct((bsz, s, nheads * MLSTM_DV), BF16 if final else F32),
        scratch_shapes=[pltpu.VMEM((MLSTM_DK, MLSTM_DV + LANES), F32),
                        pltpu.VMEM((SUBLANES, LANES), F32),
                        pltpu.VMEM((ln, MLSTM_DV + LANES), BF16)],
        compiler_params=_params(("parallel", "parallel", "arbitrary")),
        name="mlstm_fwd" if final else "mlstm_bwd",
    )(*args)


ATTN_SUB = 128


def _attn_kernel(slopes_ref, q_ref, kp_ref, kc_ref, kn_ref, vp_ref, vc_ref, vn_ref,
                 o_ref, lse_ref, kbuf, vbuf, *, tq, dil, seq, nheads):
    t = pl.program_id(1)
    colid = pl.program_id(2)
    head = colid % nheads
    half = ATTN_HALF
    nk = ATTN_SUB + 2 * half

    @pl.when(colid == 0)
    def _():
        lse_ref[...] = jnp.zeros_like(lse_ref)

    kbuf[0:half, :] = kp_ref[0]
    kbuf[half:half + tq, :] = kc_ref[0]
    kbuf[half + tq:, :] = kn_ref[0]
    vbuf[0:half, :] = vp_ref[0]
    vbuf[half:half + tq, :] = vc_ref[0]
    vbuf[half + tq:, :] = vn_ref[0]

    qi = lax.broadcasted_iota(jnp.int32, (ATTN_SUB, nk), 0)
    kj = lax.broadcasted_iota(jnp.int32, (ATTN_SUB, nk), 1)
    rel = jnp.abs(kj - half - qi)
    band = rel <= half
    bias = (-slopes_ref[head] * dil) * rel.astype(F32)
    scale = ATTN_HEAD_DIM ** -0.5
    lane = lax.broadcasted_iota(jnp.int32, (ATTN_SUB, lse_ref.shape[2]), 1)

    for j in range(tq // ATTN_SUB):
        r0 = j * ATTN_SUB
        qs = q_ref[0, r0:r0 + ATTN_SUB, :]
        ks = kbuf[r0:r0 + nk, :]
        vs = vbuf[r0:r0 + nk, :]
        s = lax.dot_general(qs, ks, (((1,), (1,)), ((), ())), preferred_element_type=F32)
        kpos = t * tq + (r0 - half) + kj
        valid = band & (kpos >= 0) & (kpos < seq)
        s = jnp.where(valid, s * scale + bias, MASK_VALUE)
        m = jnp.max(s, axis=1, keepdims=True)
        p = jnp.exp(s - m)
        l = jnp.sum(p, axis=1, keepdims=True)
        o = jnp.dot(p.astype(BF16), vs, preferred_element_type=F32)
        o_ref[0, r0:r0 + ATTN_SUB, :] = o / l
        lse = m + jnp.log(l)
        lse_ref[0, r0:r0 + ATTN_SUB, :] = jnp.where(lane == colid, lse,
                                                    lse_ref[0, r0:r0 + ATTN_SUB, :])


def _dilated_attention(slopes, q, k, v, dil):
    bsz, s, width = q.shape
    nheads = width // ATTN_HEAD_DIM
    seq = s // dil
    tq = min(512, seq)
    ncol = dil * nheads
    nhb = tq // ATTN_HALF
    last_hb = seq // ATTN_HALF - 1
    qv, kv, vv = (a.reshape(bsz, seq, dil * width) for a in (q, k, v))
    dh = ATTN_HEAD_DIM
    cur = pl.BlockSpec((1, tq, dh), lambda b, t, c, sl: (b, t, c))
    prev = pl.BlockSpec((1, ATTN_HALF, dh),
                        lambda b, t, c, sl: (b, jnp.maximum(t * nhb - 1, 0), c))
    nxt = pl.BlockSpec((1, ATTN_HALF, dh),
                       lambda b, t, c, sl: (b, jnp.minimum((t + 1) * nhb, last_hb), c))
    kern = functools.partial(_attn_kernel, tq=tq, dil=dil, seq=seq, nheads=nheads)
    o, lse = pl.pallas_call(
        kern,
        grid_spec=pltpu.PrefetchScalarGridSpec(
            num_scalar_prefetch=1,
            grid=(bsz, seq // tq, ncol),
            in_specs=[cur, prev, cur, nxt, prev, cur, nxt],
            out_specs=[pl.BlockSpec((1, tq, dh), lambda b, t, c, sl: (b, t, c)),
                       pl.BlockSpec((1, tq, ncol), lambda b, t, c, sl: (b, t, 0))],
            scratch_shapes=[pltpu.VMEM((tq + 2 * ATTN_HALF, dh), BF16),
                            pltpu.VMEM((tq + 2 * ATTN_HALF, dh), BF16)]),
        out_shape=[jax.ShapeDtypeStruct((bsz, seq, dil * width), F32),
                   jax.ShapeDtypeStruct((bsz, seq, ncol), F32)],
        compiler_params=_params(("parallel", "parallel", "arbitrary")),
        name=f"dilated_attn_d{dil}",
    )(slopes, qv, kv, kv, kv, vv, vv, vv)
    return o.reshape(bsz, s, width), lse.reshape(bsz, s, nheads)


def _attn_mix_kernel(o1_ref, o2_ref, o3_ref, l1_ref, l2_ref, l3_ref, g_ref, y_ref, *, nheads):
    l1, l2, l3 = l1_ref[0], l2_ref[0], l3_ref[0]
    mx = jnp.maximum(jnp.maximum(l1, l2), l3)
    e1, e2, e3 = jnp.exp(l1 - mx), jnp.exp(l2 - mx), jnp.exp(l3 - mx)
    inv = 1.0 / (e1 + e2 + e3)
    w1, w2, w3 = e1 * inv, e2 * inv, e3 * inv
    dh = ATTN_HEAD_DIM
    for h in range(nheads):
        sl = slice(h * dh, (h + 1) * dh)
        o = (w1[:, h:h + 1] * o1_ref[0, :, sl] + w2[:, h:h + 1] * o2_ref[0, :, sl]
             + w3[:, h:h + 1] * o3_ref[0, :, sl])
        on = o * lax.rsqrt(jnp.mean(o * o, axis=-1, keepdims=True) + RMS_EPS)
        y_ref[0, :, sl] = (on * g_ref[:, sl]).astype(y_ref.dtype)


def _attn_mix(outs, lses, head_g, ts=256):
    bsz, s, width = outs[0].shape
    nheads = width // ATTN_HEAD_DIM
    ts = min(ts, s)
    ospec = pl.BlockSpec((1, ts, width), lambda b, t: (b, t, 0))
    lspec = pl.BlockSpec((1, ts, nheads), lambda b, t: (b, t, 0))
    return pl.pallas_call(
        functools.partial(_attn_mix_kernel, nheads=nheads),
        grid=(bsz, s // ts),
        in_specs=[ospec, ospec, ospec, lspec, lspec, lspec,
                  pl.BlockSpec((1, width), lambda b, t: (0, 0))],
        out_specs=ospec,
        out_shape=jax.ShapeDtypeStruct((bsz, s, width), BF16),
        compiler_params=_params(("parallel", "parallel")),
        name="attn_mix",
    )(*outs, *lses, head_g.reshape(1, width))


def _expert_up_kernel(be_ref, nb_ref, x_ref, wg_ref, wu_ref, o_ref):
    i = pl.program_id(0)

    @pl.when(i < nb_ref[0])
    def _():
        x = x_ref[...]
        g = jnp.dot(x, wg_ref[...], preferred_element_type=F32)
        u = jnp.dot(x, wu_ref[...], preferred_element_type=F32)
        o_ref[...] = (g * jax.nn.sigmoid(g) * u).astype(o_ref.dtype)

    @pl.when(i >= nb_ref[0])
    def _():
        o_ref[...] = jnp.zeros_like(o_ref)


def _expert_down_kernel(be_ref, nb_ref, h_ref, wd_ref, sw_ref, o_ref):
    i = pl.program_id(0)

    @pl.when(i < nb_ref[0])
    def _():
        y = jnp.dot(h_ref[...], wd_ref[...], preferred_element_type=F32)
        o_ref[...] = y * sw_ref[...]

    @pl.when(i >= nb_ref[0])
    def _():
        o_ref[...] = jnp.zeros_like(o_ref)


def _expert_ffn(xs, block_e, n_used, slot_w, w_gate, w_up, w_down):
    p, d = xs.shape
    ff = w_gate.shape[2]
    nblk = p // EXPERT_BLOCK
    blk = EXPERT_BLOCK
    h1 = pl.pallas_call(
        _expert_up_kernel,
        grid_spec=pltpu.PrefetchScalarGridSpec(
            num_scalar_prefetch=2,
            grid=(nblk,),
            in_specs=[pl.BlockSpec((blk, d), lambda i, be, nb: (i, 0)),
                      pl.BlockSpec((None, d, ff), lambda i, be, nb: (be[i], 0, 0)),
                      pl.BlockSpec((None, d, ff), lambda i, be, nb: (be[i], 0, 0))],
            out_specs=pl.BlockSpec((blk, ff), lambda i, be, nb: (i, 0))),
        out_shape=jax.ShapeDtypeStruct((p, ff), BF16),
        compiler_params=_params(("arbitrary",)),
        name="expert_up",
    )(block_e, n_used, xs, w_gate, w_up)
    return pl.pallas_call(
        _expert_down_kernel,
        grid_spec=pltpu.PrefetchScalarGridSpec(
            num_scalar_prefetch=2,
            grid=(nblk,),
            in_specs=[pl.BlockSpec((blk, ff), lambda i, be, nb: (i, 0)),
                      pl.BlockSpec((None, ff, d), lambda i, be, nb: (be[i], 0, 0)),
                      pl.BlockSpec((blk, 1), lambda i, be, nb: (i, 0))],
            out_specs=pl.BlockSpec((blk, d), lambda i, be, nb: (i, 0))),
        out_shape=jax.ShapeDtypeStruct((p, d), F32),
        compiler_params=_params(("arbitrary",)),
        name="expert_down",
    )(block_e, n_used, h1, w_down, slot_w.reshape(p, 1))


def _combine_kernel(x_ref, c0_ref, c1_ref, o_ref):
    o_ref[...] = x_ref[...] + c0_ref[...] + c1_ref[...]


def _combine(x2d, c0, c1, tm=256):
    t, d = x2d.shape
    spec = pl.BlockSpec((tm, d), lambda i: (i, 0))
    return pl.pallas_call(
        _combine_kernel,
        grid=(t // tm,),
        in_specs=[spec, spec, spec],
        out_specs=spec,
        out_shape=jax.ShapeDtypeStruct((t, d), F32),
        compiler_params=_params(("parallel",)),
        name="moe_combine",
    )(x2d, c0, c1)


def _route(logits, n_groups, n_experts):
    t = logits.shape[0]
    per_group = n_experts // n_groups
    g_prob = jax.nn.softmax(logits[:, :n_groups], axis=-1)
    g_top_p, g_top = lax.top_k(g_prob, 1)
    e_logits = logits[:, n_groups:n_groups + n_experts].reshape(t, n_groups, per_group)
    e_sel = jnp.take_along_axis(e_logits, g_top[:, :, None], axis=1)[:, 0]
    e_top_p, e_top = lax.top_k(jax.nn.softmax(e_sel, axis=-1), EXPERT_TOPK)
    w_tok = g_top_p * e_top_p / jnp.sum(e_top_p, axis=-1, keepdims=True)
    expert_id = (g_top * per_group + e_top).astype(jnp.int32)

    a = t * EXPERT_TOPK
    flat_e = expert_id.reshape(a)
    flat_w = w_tok.reshape(a)
    flat_tok = jnp.repeat(jnp.arange(t, dtype=jnp.int32), EXPERT_TOPK)
    onehot = (flat_e[:, None] == jnp.arange(n_experts, dtype=jnp.int32)[None, :]).astype(jnp.int32)
    rank = jnp.take_along_axis(jnp.cumsum(onehot, axis=0), flat_e[:, None], axis=1)[:, 0] - 1
    counts = jnp.sum(onehot, axis=0)
    blocks = (counts + EXPERT_BLOCK - 1) // EXPERT_BLOCK
    blk_end = jnp.cumsum(blocks)
    pad_start = (blk_end - blocks) * EXPERT_BLOCK
    dest = pad_start[flat_e] + rank
    n_blocks = -(-a // EXPERT_BLOCK) + n_experts
    p = n_blocks * EXPERT_BLOCK
    slot_tok = jnp.zeros((p,), jnp.int32).at[dest].set(flat_tok)
    slot_w = jnp.zeros((p,), F32).at[dest].set(flat_w)
    block_e = jnp.clip(jnp.searchsorted(blk_end, jnp.arange(n_blocks), side='right'),
                       0, n_experts - 1).astype(jnp.int32)
    n_used = blk_end[-1:].astype(jnp.int32)
    return slot_tok, slot_w, block_e, n_used, dest.reshape(t, EXPERT_TOPK)


def _layer(x, norm1_g, w_in, b_gate, conv_w, conv_b, head_g, w_out, norm2_g,
           w_rg, b_rg, w_re, b_re, w_gate, w_up, w_down):
    bsz, s, d = x.shape
    t = bsz * s
    mw = d // 2
    nmh = mw // MLSTM_DV
    qkw = nmh * MLSTM_DK
    aw = d - mw
    nah = aw // ATTN_HEAD_DIM
    ngate = 4 * nmh
    x2d = x.reshape(t, d)

    offs = [0, 2 * qkw, 2 * qkw + mw, 2 * qkw + 2 * mw, 2 * qkw + 2 * mw + ngate]
    offs += [offs[4] + aw, offs[4] + 2 * aw, offs[4] + 3 * aw]
    seg = lambda i: w_in[:, offs[i]:offs[i + 1]].astype(BF16)
    w_gates = jnp.pad(w_in[:, offs[3]:offs[4]], ((0, 0), (0, LANES - ngate))).astype(BF16)

    h = _rmsnorm(x2d, norm1_g, BF16)
    qk_m = _matmul(h, seg(0), F32).reshape(bsz, s, 2 * qkw)
    v_m = _matmul(h, seg(1), BF16).reshape(bsz, s, mw)
    o_m = _matmul(h, seg(2), F32).reshape(bsz, s, mw)
    gates = _matmul(h, w_gates, F32, tn=LANES)[:, :ngate]
    q_a = _matmul(h, seg(4), BF16).reshape(bsz, s, aw)
    k_a = _matmul(h, seg(5), BF16).reshape(bsz, s, aw)
    v_a = _matmul(h, seg(6), BF16).reshape(bsz, s, aw)

    q_m = _short_conv_silu(qk_m, conv_w, conv_b, 0, qkw, MLSTM_DK ** -0.5, transpose=False)
    kt_m = _short_conv_silu(qk_m, conv_w, conv_b, qkw, qkw, 1.0, transpose=True)
    gates_t = gates.reshape(bsz, s, 4, nmh).transpose(0, 2, 3, 1).reshape(bsz, 4, nmh, 1, s)
    h_bwd = _mlstm_scan(b_gate, q_m, kt_m, v_m, gates_t, reverse=True)
    y_m = _mlstm_scan(b_gate, q_m, kt_m, v_m, gates_t, reverse=False, hprev=h_bwd, ogate=o_m,
                      head_g=head_g[:mw].reshape(1, mw))

    slopes = jnp.exp2(-8.0 * jnp.arange(1, nah + 1, dtype=F32) / nah)
    outs, lses = [], []
    for _, dil in DILATED_PATTERNS:
        o_p, l_p = _dilated_attention(slopes, q_a, k_a, v_a, dil)
        outs.append(o_p)
        lses.append(l_p)
    y_a = _attn_mix(outs, lses, head_g[mw:])

    w_out_b = w_out.astype(BF16)
    x2d = _out_proj(y_m.reshape(t, mw), y_a.reshape(t, aw), w_out_b[:mw], w_out_b[mw:], x2d)

    n_groups = w_rg.shape[1]
    n_experts = w_re.shape[1]
    w_router = jnp.pad(jnp.concatenate([w_rg, w_re], axis=1),
                       ((0, 0), (0, ROUTER_LANES - n_groups - n_experts)))
    b_router = jnp.pad(jnp.concatenate([b_rg, b_re]),
                       (0, ROUTER_LANES - n_groups - n_experts)).reshape(1, ROUTER_LANES)
    h2, logits = _rmsnorm_router(x2d, norm2_g, w_router, b_router)
    slot_tok, slot_w, block_e, n_used, pos = _route(logits, n_groups, n_experts)
    xs = jnp.take(h2, slot_tok, axis=0)
    contrib = _expert_ffn(xs, block_e, n_used, slot_w,
                          w_gate.astype(BF16), w_up.astype(BF16), w_down.astype(BF16))
    x2d = _combine(x2d, jnp.take(contrib, pos[:, 0], axis=0), jnp.take(contrib, pos[:, 1], axis=0))
    return x2d.reshape(bsz, s, d)


def kernel(x, norm1_g, w_in, b_gate, conv_w, conv_b, head_norm_g, w_out, norm2_g, w_router_group,
           b_router_group, w_router_expert, b_router_expert, w_gate, w_up, w_down, final_norm_g):
    depth = norm1_g.shape[0]
    for l in range(depth):
        x = _layer(x, norm1_g[l], w_in[l], b_gate[l], conv_w[l], conv_b[l], head_norm_g[l],
                   w_out[l], norm2_g[l], w_router_group[l], b_router_group[l],
                   w_router_expert[l], b_router_expert[l], w_gate[l], w_up[l], w_down[l])
    bsz, s, d = x.shape
    return _rmsnorm(x.reshape(bsz * s, d), final_norm_g, x.dtype).reshape(bsz, s, d)
```

```python
import functools

import jax
import jax.numpy as jnp
from jax import lax
from jax.experimental import pallas as pl
from jax.experimental.pallas import tpu as pltpu

MLSTM_DV = 512
MLSTM_DK = 256
ATTN_HEAD_DIM = 128
CONV_W = 5
DILATED_PATTERNS = ((128, 1), (512, 4), (2048, 16))
ATTN_HALF = 64
EXPERT_TOPK = 2
RMS_EPS = 1e-6
MASK_VALUE = -1e30

LANES = 128
SUBLANES = 8
VMEM_LIMIT_BYTES = 56 * 1024 * 1024

MLSTM_CHUNK = 256
EXPERT_BLOCK = 256
ROUTER_ROWS = 32

BF16 = jnp.bfloat16
F32 = jnp.float32


def _params(semantics):
    return pltpu.CompilerParams(dimension_semantics=semantics, vmem_limit_bytes=VMEM_LIMIT_BYTES)


def _rmsnorm_kernel(x_ref, g_ref, o_ref):
    x = x_ref[...]
    ms = jnp.mean(x * x, axis=-1, keepdims=True)
    o_ref[...] = (x * lax.rsqrt(ms + RMS_EPS) * g_ref[...]).astype(o_ref.dtype)


def _rmsnorm(x2d, g, out_dtype, tm=256):
    t, d = x2d.shape
    return pl.pallas_call(
        _rmsnorm_kernel,
        grid=(t // tm,),
        in_specs=[pl.BlockSpec((tm, d), lambda i: (i, 0)),
                  pl.BlockSpec((1, d), lambda i: (0, 0))],
        out_specs=pl.BlockSpec((tm, d), lambda i: (i, 0)),
        out_shape=jax.ShapeDtypeStruct((t, d), out_dtype),
        compiler_params=_params(("parallel",)),
        name="rmsnorm",
    )(x2d, g.reshape(1, d))


def _first_argmax(vals):
    best = vals[0]
    idx = jnp.zeros(best.shape, jnp.int32)
    for i in range(1, len(vals)):
        gt = vals[i] > best
        best = jnp.where(gt, vals[i], best)
        idx = jnp.where(gt, i, idx)
    return best, idx


def _softmax_list(vals):
    mx = functools.reduce(jnp.maximum, vals)
    es = [jnp.exp(v - mx) for v in vals]
    tot = functools.reduce(lambda a, b: a + b, es)
    return [e / tot for e in es]


def _rmsnorm_router_kernel(x_ref, g_ref, wrt_ref, brt_ref, h_ref, eid_ref, wt_ref,
                           *, n_groups, per_group):
    x = x_ref[...]
    ms = jnp.mean(x * x, axis=-1, keepdims=True)
    h = x * lax.rsqrt(ms + RMS_EPS) * g_ref[...]
    h_ref[...] = h.astype(h_ref.dtype)
    lt = lax.dot_general(wrt_ref[...], h, (((1,), (1,)), ((), ())),
                         precision=lax.Precision.HIGHEST, preferred_element_type=F32) + brt_ref[...]
    g_prob = _softmax_list([lt[i:i + 1, :] for i in range(n_groups)])
    g_top_p, g_top = _first_argmax(g_prob)
    e_sel = []
    for j in range(per_group):
        row = lt[n_groups + j:n_groups + j + 1, :]
        for grp in range(1, n_groups):
            r = n_groups + grp * per_group + j
            row = jnp.where(g_top == grp, lt[r:r + 1, :], row)
        e_sel.append(row)
    e_prob = _softmax_list(e_sel)
    p1, i1 = _first_argmax(e_prob)
    p2, i2 = _first_argmax([jnp.where(i1 == j, -1.0, e_prob[j]) for j in range(per_group)])
    denom = p1 + p2
    w1 = g_top_p * p1 / denom
    w2 = g_top_p * p2 / denom
    e1 = g_top * per_group + i1
    e2 = g_top * per_group + i2
    rows = lax.broadcasted_iota(jnp.int32, eid_ref.shape, 0)
    eid_ref[...] = jnp.where(rows == 0, e1, jnp.where(rows == 1, e2, 0))
    wt_ref[...] = jnp.where(rows == 0, w1, jnp.where(rows == 1, w2, 0.0))


def _rmsnorm_router(x2d, g, w_router_t, b_router_t, n_groups, per_group, tm=256):
    t, d = x2d.shape
    kern = functools.partial(_rmsnorm_router_kernel, n_groups=n_groups, per_group=per_group)
    return pl.pallas_call(
        kern,
        grid=(t // tm,),
        in_specs=[pl.BlockSpec((tm, d), lambda i: (i, 0)),
                  pl.BlockSpec((1, d), lambda i: (0, 0)),
                  pl.BlockSpec((ROUTER_ROWS, d), lambda i: (0, 0)),
                  pl.BlockSpec((ROUTER_ROWS, 1), lambda i: (0, 0))],
        out_specs=[pl.BlockSpec((tm, d), lambda i: (i, 0)),
                   pl.BlockSpec((SUBLANES, tm), lambda i: (0, i)),
                   pl.BlockSpec((SUBLANES, tm), lambda i: (0, i))],
        out_shape=[jax.ShapeDtypeStruct((t, d), BF16),
                   jax.ShapeDtypeStruct((SUBLANES, t), jnp.int32),
                   jax.ShapeDtypeStruct((SUBLANES, t), F32)],
        compiler_params=_params(("parallel",)),
        name="rmsnorm_router",
    )(x2d, g.reshape(1, d), w_router_t, b_router_t)


def _matmul_kernel(a_ref, b_ref, o_ref, bb_ref):
    @pl.when(pl.program_id(1) == 0)
    def _():
        bb_ref[...] = b_ref[...].astype(BF16)

    o_ref[...] = jnp.dot(a_ref[...], bb_ref[...], preferred_element_type=F32).astype(o_ref.dtype)


def _matmul(a, b, out_dtype, col0=0, n=None, tm=512, tn=512):
    m, k = a.shape
    n = b.shape[1] if n is None else n
    tn = min(tn, n)
    tm = min(tm, m)
    cb0 = col0 // tn
    return pl.pallas_call(
        _matmul_kernel,
        grid=(n // tn, m // tm),
        in_specs=[pl.BlockSpec((tm, k), lambda j, i: (i, 0)),
                  pl.BlockSpec((k, tn), lambda j, i: (0, cb0 + j))],
        out_specs=pl.BlockSpec((tm, tn), lambda j, i: (i, j)),
        out_shape=jax.ShapeDtypeStruct((m, n), out_dtype),
        scratch_shapes=[pltpu.VMEM((k, tn), BF16)],
        compiler_params=_params(("parallel", "arbitrary")),
        name="proj_in",
    )(a, b)


def _out_proj_kernel(a1_ref, a2_ref, b1_ref, b2_ref, r_ref, o_ref, bb1_ref, bb2_ref):
    @pl.when(pl.program_id(1) == 0)
    def _():
        bb1_ref[...] = b1_ref[...].astype(BF16)
        bb2_ref[...] = b2_ref[...].astype(BF16)

    acc = jnp.dot(a1_ref[...], bb1_ref[...], preferred_element_type=F32)
    acc = acc + jnp.dot(a2_ref[...], bb2_ref[...], preferred_element_type=F32)
    o_ref[...] = r_ref[...] + acc


def _out_proj(a1, a2, w, res, tm=512, tn=512):
    m, k1 = a1.shape
    k2 = a2.shape[1]
    n = w.shape[1]
    tn = min(tn, n)
    tm = min(tm, m)
    assert k1 == k2
    return pl.pallas_call(
        _out_proj_kernel,
        grid=(n // tn, m // tm),
        in_specs=[pl.BlockSpec((tm, k1), lambda j, i: (i, 0)),
                  pl.BlockSpec((tm, k2), lambda j, i: (i, 0)),
                  pl.BlockSpec((k1, tn), lambda j, i: (0, j)),
                  pl.BlockSpec((k2, tn), lambda j, i: (1, j)),
                  pl.BlockSpec((tm, tn), lambda j, i: (i, j))],
        out_specs=pl.BlockSpec((tm, tn), lambda j, i: (i, j)),
        out_shape=jax.ShapeDtypeStruct((m, n), F32),
        scratch_shapes=[pltpu.VMEM((k1, tn), BF16), pltpu.VMEM((k2, tn), BF16)],
        compiler_params=_params(("parallel", "arbitrary")),
        name="proj_out",
    )(a1, a2, w, w, res)


def _conv_kernel(up_ref, uc_ref, un_ref, w_ref, b_ref, o_ref, buf_ref, *, ts, scale, transpose):
    t = pl.program_id(2)
    nt = pl.num_programs(2)
    pad = SUBLANES
    buf_ref[0:pad, :] = jnp.where(t > 0, up_ref[0], 0.0)
    buf_ref[pad:pad + ts, :] = uc_ref[0]
    buf_ref[pad + ts:2 * pad + ts, :] = jnp.where(t < nt - 1, un_ref[0], 0.0)
    acc = jnp.zeros((ts, buf_ref.shape[1]), F32) + b_ref[...]
    for j in range(CONV_W):
        acc = acc + buf_ref[pl.ds(pad - CONV_W // 2 + j, ts), :] * w_ref[j:j + 1, :]
    y = acc * jax.nn.sigmoid(acc) * scale
    if transpose:
        o_ref[0] = y.T.astype(o_ref.dtype)
    else:
        o_ref[0] = y.astype(o_ref.dtype)


def _short_conv_silu(u, conv_w, conv_b, col0, width, scale, transpose, ts=512):
    bsz, s, _ = u.shape
    tc = MLSTM_DK
    ts = min(ts, s)
    cb0 = col0 // tc
    nrb = ts // SUBLANES
    last_rb = s // SUBLANES - 1
    kern = functools.partial(_conv_kernel, ts=ts, scale=scale, transpose=transpose)
    if transpose:
        out_spec = pl.BlockSpec((1, tc, ts), lambda b, c, t: (b, c, t))
        out_shape = jax.ShapeDtypeStruct((bsz, width, s), BF16)
    else:
        out_spec = pl.BlockSpec((1, ts, tc), lambda b, c, t: (b, t, c))
        out_shape = jax.ShapeDtypeStruct((bsz, s, width), BF16)
    return pl.pallas_call(
        kern,
        grid=(bsz, width // tc, s // ts),
        in_specs=[pl.BlockSpec((1, SUBLANES, tc),
                               lambda b, c, t: (b, jnp.maximum(t * nrb - 1, 0), cb0 + c)),
                  pl.BlockSpec((1, ts, tc), lambda b, c, t: (b, t, cb0 + c)),
                  pl.BlockSpec((1, SUBLANES, tc),
                               lambda b, c, t: (b, jnp.minimum((t + 1) * nrb, last_rb), cb0 + c)),
                  pl.BlockSpec((CONV_W, tc), lambda b, c, t: (0, cb0 + c)),
                  pl.BlockSpec((1, tc), lambda b, c, t: (0, cb0 + c))],
        out_specs=out_spec,
        out_shape=out_shape,
        scratch_shapes=[pltpu.VMEM((ts + 2 * SUBLANES, tc), F32)],
        compiler_params=_params(("parallel", "parallel", "parallel")),
        name="conv_silu_t" if transpose else "conv_silu",
    )(u, u, u, conv_w, conv_b.reshape(1, -1))


def _log_sigmoid(x):
    return jnp.minimum(x, 0.0) - jnp.log1p(jnp.exp(-jnp.abs(x)))


def _mlstm_kernel(bias_ref, q_ref, kt_ref, v_ref, g_ref, *rest, chunk, nheads, reverse, final):
    if final:
        hprev_ref, og_ref, hg_ref, o_ref, ct_ref, m_ref, vext_ref = rest
    else:
        o_ref, ct_ref, m_ref, vext_ref = rest
    h_idx = pl.program_id(1)
    c_idx = pl.program_id(2)
    dv = MLSTM_DV
    ln = chunk

    @pl.when(c_idx == 0)
    def _():
        ct_ref[...] = jnp.zeros_like(ct_ref)
        m_ref[...] = jnp.zeros_like(m_ref)

    gi, gf = (2, 3) if reverse else (0, 1)
    i_row = g_ref[gi] + bias_ref[gi * nheads + h_idx]
    f_row = _log_sigmoid(g_ref[gf] + bias_ref[gf * nheads + h_idx])

    row = lax.broadcasted_iota(jnp.int32, (ln, ln), 0)
    col = lax.broadcasted_iota(jnp.int32, (ln, ln), 1)
    cum_mask = (row >= col) if reverse else (row <= col)
    b_row = jnp.dot(jnp.broadcast_to(f_row, (SUBLANES, ln)), cum_mask.astype(F32),
                    precision=lax.Precision.HIGHEST, preferred_element_type=F32)[0:1, :]
    total = jnp.sum(f_row, axis=1, keepdims=True)
    b_col = jnp.sum(jnp.where(row == col, b_row, 0.0), axis=1, keepdims=True)

    m0 = m_ref[0:1, 0:1]
    tri = (col >= row) if reverse else (col <= row)
    d_log = jnp.where(tri, b_col - b_row + i_row, MASK_VALUE)
    a_col = b_col + m0
    m_t = jnp.maximum(a_col, jnp.max(d_log, axis=1, keepdims=True))
    w_intra = jnp.exp(d_log - m_t)
    w_inter = jnp.exp(a_col - m_t)

    q = q_ref[0]
    kt = kt_ref[0]
    vext_ref[:, 0:dv] = v_ref[0]
    vext_ref[:, dv:] = jnp.ones((ln, LANES), vext_ref.dtype)
    vext = vext_ref[...]

    s_qk = jnp.dot(q, kt, preferred_element_type=F32) * w_intra
    intra = jnp.dot(s_qk.astype(BF16), vext, preferred_element_type=F32)
    ct = ct_ref[...]
    inter = jnp.dot(q, ct.astype(BF16), preferred_element_type=F32)
    tot = intra + w_inter * inter
    den = tot[:, dv:dv + 1]
    h = tot[:, 0:dv] / jnp.maximum(jnp.abs(den), jnp.exp(-m_t))

    g_row = total - b_row + i_row
    m_new = jnp.maximum(total + m0, jnp.max(g_row, axis=1, keepdims=True))
    decay = jnp.exp(total + m0 - m_new)
    w_add = jnp.exp(g_row - m_new)
    ktw = (kt.astype(F32) * w_add).astype(BF16)
    ct_ref[...] = decay * ct + jnp.dot(ktw, vext, preferred_element_type=F32)
    m_ref[...] = jnp.broadcast_to(m_new, m_ref.shape)

    if final:
        hs = h + hprev_ref[0]
        hn = hs * lax.rsqrt(jnp.mean(hs * hs, axis=-1, keepdims=True) + RMS_EPS)
        o_ref[0] = (hn * hg_ref[...] * jax.nn.sigmoid(og_ref[0])).astype(o_ref.dtype)
    else:
        o_ref[0] = h


def _mlstm_scan(bias, q, kt, v, gates_t, reverse, hprev=None, ogate=None, head_g=None):
    bsz, s, _ = q.shape
    nheads = v.shape[2] // MLSTM_DV
    ln = min(MLSTM_CHUNK, s)
    nc = s // ln
    final = hprev is not None

    def cidx(c):
        return (nc - 1 - c) if reverse else c

    in_specs = [pl.BlockSpec(memory_space=pltpu.SMEM),
                pl.BlockSpec((1, ln, MLSTM_DK), lambda b, h, c: (b, cidx(c), h)),
                pl.BlockSpec((1, MLSTM_DK, ln), lambda b, h, c: (b, h, cidx(c))),
                pl.BlockSpec((1, ln, MLSTM_DV), lambda b, h, c: (b, cidx(c), h)),
                pl.BlockSpec((None, 4, None, 1, ln), lambda b, h, c: (b, 0, h, 0, cidx(c)))]
    args = [bias, q, kt, v, gates_t]
    if final:
        in_specs += [pl.BlockSpec((1, ln, MLSTM_DV), lambda b, h, c: (b, cidx(c), h)),
                     pl.BlockSpec((1, ln, MLSTM_DV), lambda b, h, c: (b, cidx(c), h)),
                     pl.BlockSpec((1, MLSTM_DV), lambda b, h, c: (0, h))]
        args += [hprev, ogate, head_g]
    kern = functools.partial(_mlstm_kernel, chunk=ln, nheads=nheads, reverse=reverse, final=final)
    return pl.pallas_call(
        kern,
        grid=(bsz, nheads, nc),
        in_specs=in_specs,
        out_specs=pl.BlockSpec((1, ln, MLSTM_DV), lambda b, h, c: (b, cidx(c), h)),
        out_shape=jax.ShapeDtypeStruct((bsz, s, nheads * MLSTM_DV), BF16 if final else F32),
        scratch_shapes=[pltpu.VMEM((MLSTM_DK, MLSTM_DV + LANES), F32),
                        pltpu.VMEM((SUBLANES, LANES), F32),
                        pltpu.VMEM((ln, MLSTM_DV + LANES), BF16)],
        compiler_params=_params(("parallel", "parallel", "arbitrary")),
        name="mlstm_fwd" if final else "mlstm_bwd",
    )(*args)


ATTN_TILE = 1024
ATTN_HALO = ATTN_HALF * 16
ATTN_HEADS_PER_STEP = 2
ATTN_SUB = 128


def _band_bias(nq, nk, slope_d):
    qi = lax.broadcasted_iota(jnp.int32, (nq, nk), 0)
    kj = lax.broadcasted_iota(jnp.int32, (nq, nk), 1)
    rel = jnp.abs(kj - ATTN_HALF - qi)
    return jnp.where(rel <= ATTN_HALF, -slope_d * rel.astype(F32), MASK_VALUE)


def _band_scores(qs, ks, vs, bias, key_lo, key_hi):
    s = lax.dot_general(qs.astype(BF16), ks.astype(BF16), (((1,), (1,)), ((), ())),
                        preferred_element_type=F32) + bias
    if key_lo is not None or key_hi is not None:
        kj = lax.broadcasted_iota(jnp.int32, s.shape, 1)
        inside = None
        if key_lo is not None:
            inside = kj >= key_lo
        if key_hi is not None:
            inside = (kj < key_hi) if inside is None else (inside & (kj < key_hi))
        s = jnp.where(inside, s, MASK_VALUE)
    m = jnp.max(s, axis=1, keepdims=True)
    p = jnp.exp(s - m)
    l = jnp.sum(p, axis=1, keepdims=True)
    o = jnp.dot(p.astype(BF16), vs.astype(BF16), preferred_element_type=F32) / l
    return o, m + jnp.log(l)


def _attn_kernel(slopes_ref, q_ref, kp_ref, kc_ref, kn_ref, vp_ref, vc_ref, vn_ref, g_ref,
                 y_ref, qbuf, kwin, vwin, acc_ref, lse_ref, *, tq):
    t = pl.program_id(1)
    cblk = pl.program_id(2)
    halo = ATTN_HALO
    half = ATTN_HALF
    dh = ATTN_HEAD_DIM
    first_tile = t == 0
    last_tile = t == pl.num_programs(1) - 1

    for hh in range(ATTN_HEADS_PER_STEP):
        sl = slice(hh * dh, (hh + 1) * dh)
        slope = slopes_ref[cblk * ATTN_HEADS_PER_STEP + hh]
        qbuf[...] = q_ref[0, :, sl] * (ATTN_HEAD_DIM ** -0.5)
        kwin[0:halo, :] = kp_ref[0, :, sl]
        kwin[halo:halo + tq, :] = kc_ref[0, :, sl]
        kwin[halo + tq:, :] = kn_ref[0, :, sl]
        vwin[0:halo, :] = vp_ref[0, :, sl]
        vwin[halo:halo + tq, :] = vc_ref[0, :, sl]
        vwin[halo + tq:, :] = vn_ref[0, :, sl]

        for p_idx, (_, dil) in enumerate(DILATED_PATTERNS):
            n_class = tq // dil
            nsub = min(ATTN_SUB, n_class)
            nk = nsub + 2 * half
            nblk = n_class // nsub
            bias = _band_bias(nsub, nk, slope * dil)
            key_lo = jnp.where(first_tile, half, 0)
            key_hi = jnp.where(last_tile, nsub + half, nk)
            for r in range(dil):
                for j in range(nblk):
                    q0 = r + dil * (j * nsub)
                    k0 = halo + r + dil * (j * nsub - half)
                    if dil == 1:
                        rows_q, rows_k = pl.ds(q0, nsub), pl.ds(k0, nk)
                    else:
                        rows_q = pl.ds(q0, nsub, stride=dil)
                        rows_k = pl.ds(k0, nk, stride=dil)
                    o, lse = _band_scores(qbuf[rows_q, :], kwin[rows_k, :], vwin[rows_k, :], bias,
                                          key_lo if j == 0 else None,
                                          key_hi if j == nblk - 1 else None)
                    acc_ref[p_idx, rows_q, :] = o
                    lse_ref[p_idx, rows_q, :] = jnp.broadcast_to(lse, (nsub, dh))

        l1, l2, l3 = lse_ref[0], lse_ref[1], lse_ref[2]
        mx = jnp.maximum(jnp.maximum(l1, l2), l3)
        e1, e2, e3 = jnp.exp(l1 - mx), jnp.exp(l2 - mx), jnp.exp(l3 - mx)
        o = (e1 * acc_ref[0] + e2 * acc_ref[1] + e3 * acc_ref[2]) / (e1 + e2 + e3)
        on = o * lax.rsqrt(jnp.mean(o * o, axis=-1, keepdims=True) + RMS_EPS)
        y_ref[0, :, sl] = (on * g_ref[:, sl]).astype(y_ref.dtype)


def _dilated_attention(slopes, q, k, v, head_g):
    bsz, s, width = q.shape
    tq = min(ATTN_TILE, s)
    halo = ATTN_HALO
    assert tq % halo == 0 and s % tq == 0
    hb = ATTN_HEADS_PER_STEP * ATTN_HEAD_DIM
    nhb = tq // halo
    last_hb = s // halo - 1
    cur = pl.BlockSpec((1, tq, hb), lambda b, t, c, sl: (b, t, c))
    prev = pl.BlockSpec((1, halo, hb), lambda b, t, c, sl: (b, jnp.maximum(t * nhb - 1, 0), c))
    nxt = pl.BlockSpec((1, halo, hb),
                       lambda b, t, c, sl: (b, jnp.minimum((t + 1) * nhb, last_hb), c))
    kern = functools.partial(_attn_kernel, tq=tq)
    dh = ATTN_HEAD_DIM
    return pl.pallas_call(
        kern,
        grid_spec=pltpu.PrefetchScalarGridSpec(
            num_scalar_prefetch=1,
            grid=(bsz, s // tq, width // hb),
            in_specs=[cur, prev, cur, nxt, prev, cur, nxt,
                      pl.BlockSpec((1, hb), lambda b, t, c, sl: (0, c))],
            out_specs=pl.BlockSpec((1, tq, hb), lambda b, t, c, sl: (b, t, c)),
            scratch_shapes=[pltpu.VMEM((tq, dh), F32),
                            pltpu.VMEM((tq + 2 * halo, dh), F32),
                            pltpu.VMEM((tq + 2 * halo, dh), F32),
                            pltpu.VMEM((len(DILATED_PATTERNS), tq, dh), F32),
                            pltpu.VMEM((len(DILATED_PATTERNS), tq, dh), F32)]),
        out_shape=jax.ShapeDtypeStruct((bsz, s, width), BF16),
        compiler_params=_params(("parallel", "parallel", "parallel")),
        name="dilated_attn",
    )(slopes, q, k, k, k, v, v, v, head_g.reshape(1, width))


def _expert_up_kernel(be_ref, nb_ref, x_ref, wg_ref, wu_ref, o_ref):
    i = pl.program_id(0)

    @pl.when(i < nb_ref[0])
    def _():
        x = x_ref[...]
        g = jnp.dot(x, wg_ref[...], preferred_element_type=F32)
        u = jnp.dot(x, wu_ref[...], preferred_element_type=F32)
        o_ref[...] = (g * jax.nn.sigmoid(g) * u).astype(o_ref.dtype)

    @pl.when(i >= nb_ref[0])
    def _():
        o_ref[...] = jnp.zeros_like(o_ref)


def _expert_down_kernel(be_ref, nb_ref, h_ref, wd_ref, o_ref):
    i = pl.program_id(0)

    @pl.when(i < nb_ref[0])
    def _():
        o_ref[...] = jnp.dot(h_ref[...], wd_ref[...], preferred_element_type=F32)

    @pl.when(i >= nb_ref[0])
    def _():
        o_ref[...] = jnp.zeros_like(o_ref)


def _expert_ffn(xs, block_e, n_used, w_gate, w_up, w_down):
    p, d = xs.shape
    ff = w_gate.shape[2]
    nblk = p // EXPERT_BLOCK
    blk = EXPERT_BLOCK
    h1 = pl.pallas_call(
        _expert_up_kernel,
        grid_spec=pltpu.PrefetchScalarGridSpec(
            num_scalar_prefetch=2,
            grid=(nblk,),
            in_specs=[pl.BlockSpec((blk, d), lambda i, be, nb: (i, 0)),
                      pl.BlockSpec((None, d, ff), lambda i, be, nb: (be[i], 0, 0)),
                      pl.BlockSpec((None, d, ff), lambda i, be, nb: (be[i], 0, 0))],
            out_specs=pl.BlockSpec((blk, ff), lambda i, be, nb: (i, 0))),
        out_shape=jax.ShapeDtypeStruct((p, ff), BF16),
        compiler_params=_params(("arbitrary",)),
        name="expert_up",
    )(block_e, n_used, xs, w_gate, w_up)
    return pl.pallas_call(
        _expert_down_kernel,
        grid_spec=pltpu.PrefetchScalarGridSpec(
            num_scalar_prefetch=2,
            grid=(nblk,),
            in_specs=[pl.BlockSpec((blk, ff), lambda i, be, nb: (i, 0)),
                      pl.BlockSpec((None, ff, d), lambda i, be, nb: (be[i], 0, 0))],
            out_specs=pl.BlockSpec((blk, d), lambda i, be, nb: (i, 0))),
        out_shape=jax.ShapeDtypeStruct((p, d), F32),
        compiler_params=_params(("arbitrary",)),
        name="expert_down",
    )(block_e, n_used, h1, w_down)


def _combine_kernel(x_ref, c0_ref, c1_ref, w0_ref, w1_ref, o_ref):
    o_ref[...] = x_ref[...] + w0_ref[...] * c0_ref[...] + w1_ref[...] * c1_ref[...]


def _combine(x2d, c0, c1, w0, w1, tm=256):
    t, d = x2d.shape
    spec = pl.BlockSpec((tm, d), lambda i: (i, 0))
    wspec = pl.BlockSpec((tm, 1), lambda i: (i, 0))
    return pl.pallas_call(
        _combine_kernel,
        grid=(t // tm,),
        in_specs=[spec, spec, spec, wspec, wspec],
        out_specs=spec,
        out_shape=jax.ShapeDtypeStruct((t, d), F32),
        compiler_params=_params(("parallel",)),
        name="moe_combine",
    )(x2d, c0, c1, w0, w1)


def _route(eid, n_experts):
    t = eid.shape[1]
    a = EXPERT_TOPK * t
    flat_e = eid.reshape(a)
    onehot = (flat_e[:, None] == jnp.arange(n_experts, dtype=jnp.int32)[None, :]).astype(jnp.int32)
    csum = jnp.cumsum(onehot, axis=0)
    counts = csum[-1]
    blocks = (counts + EXPERT_BLOCK - 1) // EXPERT_BLOCK
    blk_end = jnp.cumsum(blocks)
    pad_start = (blk_end - blocks) * EXPERT_BLOCK
    dest = jnp.sum(onehot * (csum - 1 + pad_start[None, :]), axis=1)
    n_blocks = -(-a // EXPERT_BLOCK) + n_experts
    p = n_blocks * EXPERT_BLOCK
    tok = jnp.arange(a, dtype=jnp.int32) % t
    slot_tok = jnp.zeros((p,), jnp.int32).at[dest].set(tok)
    block_e = jnp.sum((blk_end[None, :] <= jnp.arange(n_blocks, dtype=jnp.int32)[:, None])
                      .astype(jnp.int32), axis=1)
    block_e = jnp.minimum(block_e, n_experts - 1)
    n_used = blk_end[-1:].astype(jnp.int32)
    return slot_tok, block_e, n_used, dest.reshape(EXPERT_TOPK, t)


def _layer(x, norm1_g, w_in, b_gate, conv_w, conv_b, head_g, w_out, norm2_g,
           w_rg, b_rg, w_re, b_re, w_gate, w_up, w_down):
    bsz, s, d = x.shape
    t = bsz * s
    mw = d // 2
    nmh = mw // MLSTM_DV
    qkw = nmh * MLSTM_DK
    aw = d - mw
    nah = aw // ATTN_HEAD_DIM
    ngate = 4 * nmh
    x2d = x.reshape(t, d)

    c_v = 2 * qkw
    c_o = c_v + mw
    c_g = c_o + mw
    c_a = c_g + ngate
    w_gates = jnp.pad(w_in[:, c_g:c_a], ((0, 0), (0, LANES - ngate)))
    w_attn = w_in[:, c_a:]

    h = _rmsnorm(x2d, norm1_g, BF16)
    qk_m = _matmul(h, w_in, F32, col0=0, n=2 * qkw).reshape(bsz, s, 2 * qkw)
    v_m = _matmul(h, w_in, BF16, col0=c_v, n=mw).reshape(bsz, s, mw)
    o_m = _matmul(h, w_in, F32, col0=c_o, n=mw).reshape(bsz, s, mw)
    gates = _matmul(h, w_gates, F32)[:, :ngate]
    q_a = _matmul(h, w_attn, F32, col0=0, n=aw).reshape(bsz, s, aw)
    k_a = _matmul(h, w_attn, F32, col0=aw, n=aw).reshape(bsz, s, aw)
    v_a = _matmul(h, w_attn, F32, col0=2 * aw, n=aw).reshape(bsz, s, aw)

    q_m = _short_conv_silu(qk_m, conv_w, conv_b, 0, qkw, MLSTM_DK ** -0.5, transpose=False)
    kt_m = _short_conv_silu(qk_m, conv_w, conv_b, qkw, qkw, 1.0, transpose=True)
    gates_t = gates.reshape(bsz, s, 4, nmh).transpose(0, 2, 3, 1).reshape(bsz, 4, nmh, 1, s)
    h_bwd = _mlstm_scan(b_gate, q_m, kt_m, v_m, gates_t, reverse=True)
    y_m = _mlstm_scan(b_gate, q_m, kt_m, v_m, gates_t, reverse=False, hprev=h_bwd, ogate=o_m,
                      head_g=head_g[:mw].reshape(1, mw))

    slopes = jnp.exp2(-8.0 * jnp.arange(1, nah + 1, dtype=F32) / nah)
    y_a = _dilated_attention(slopes, q_a, k_a, v_a, head_g[mw:])

    x2d = _out_proj(y_m.reshape(t, mw), y_a.reshape(t, aw), w_out, x2d)

    n_groups = w_rg.shape[1]
    n_experts = w_re.shape[1]
    npad = ROUTER_ROWS - n_groups - n_experts
    w_router_t = jnp.pad(jnp.concatenate([w_rg, w_re], axis=1).T, ((0, npad), (0, 0)))
    b_router_t = jnp.pad(jnp.concatenate([b_rg, b_re]), (0, npad)).reshape(ROUTER_ROWS, 1)
    h2, eid, wts = _rmsnorm_router(x2d, norm2_g, w_router_t, b_router_t,
                                   n_groups, n_experts // n_groups)
    slot_tok, block_e, n_used, pos = _route(eid[:EXPERT_TOPK], n_experts)
    xs = jnp.take(h2, slot_tok, axis=0)
    yb = _expert_ffn(xs, block_e, n_used,
                     w_gate.astype(BF16), w_up.astype(BF16), w_down.astype(BF16))
    x2d = _combine(x2d, jnp.take(yb, pos[0], axis=0), jnp.take(yb, pos[1], axis=0),
                   wts[0].reshape(t, 1), wts[1].reshape(t, 1))
    return x2d.reshape(bsz, s, d)


def kernel(x, norm1_g, w_in, b_gate, conv_w, conv_b, head_norm_g, w_out, norm2_g, w_router_group,
           b_router_group, w_router_expert, b_router_expert, w_gate, w_up, w_down, final_norm_g):
    depth = norm1_g.shape[0]
    for l in range(depth):
        x = _layer(x, norm1_g[l], w_in[l], b_gate[l], conv_w[l], conv_b[l], head_norm_g[l],
                   w_out[l], norm2_g[l], w_router_group[l], b_router_group[l],
                   w_router_expert[l], b_router_expert[l], w_gate[l], w_up[l], w_down[l])
    bsz, s, d = x.shape
    return _rmsnorm(x.reshape(bsz * s, d), final_norm_g, x.dtype).reshape(bsz, s, d)
```

```python
import functools

import jax
import jax.numpy as jnp
from jax import lax
from jax.experimental import pallas as pl
from jax.experimental.pallas import tpu as pltpu

MLSTM_DV = 512
MLSTM_DK = 256
ATTN_HEAD_DIM = 128
CONV_W = 5
DILATED_PATTERNS = ((128, 1), (512, 4), (2048, 16))
ATTN_HALF = 64
EXPERT_TOPK = 2
RMS_EPS = 1e-6
MASK_VALUE = -1e30

LANES = 128
SUBLANES = 8
VMEM_LIMIT_BYTES = 56 * 1024 * 1024

MLSTM_CHUNK = 256
EXPERT_BLOCK = 256
ROUTER_ROWS = 32

BF16 = jnp.bfloat16
F32 = jnp.float32


def _params(semantics):
    return pltpu.CompilerParams(dimension_semantics=semantics, vmem_limit_bytes=VMEM_LIMIT_BYTES)


def _rmsnorm_kernel(x_ref, g_ref, o_ref):
    x = x_ref[...]
    ms = jnp.mean(x * x, axis=-1, keepdims=True)
    o_ref[...] = (x * lax.rsqrt(ms + RMS_EPS) * g_ref[...]).astype(o_ref.dtype)


def _rmsnorm(x2d, g, out_dtype, tm=256):
    t, d = x2d.shape
    return pl.pallas_call(
        _rmsnorm_kernel,
        grid=(t // tm,),
        in_specs=[pl.BlockSpec((tm, d), lambda i: (i, 0)),
                  pl.BlockSpec((1, d), lambda i: (0, 0))],
        out_specs=pl.BlockSpec((tm, d), lambda i: (i, 0)),
        out_shape=jax.ShapeDtypeStruct((t, d), out_dtype),
        compiler_params=_params(("parallel",)),
        name="rmsnorm",
    )(x2d, g.reshape(1, d))


def _first_argmax(vals):
    best = vals[0]
    idx = jnp.zeros(best.shape, jnp.int32)
    for i in range(1, len(vals)):
        gt = vals[i] > best
        best = jnp.where(gt, vals[i], best)
        idx = jnp.where(gt, i, idx)
    return best, idx


def _softmax_list(vals):
    mx = functools.reduce(jnp.maximum, vals)
    es = [jnp.exp(v - mx) for v in vals]
    tot = functools.reduce(lambda a, b: a + b, es)
    return [e / tot for e in es]


def _rmsnorm_router_kernel(x_ref, g_ref, wrt_ref, brt_ref, h_ref, eid_ref, wt_ref,
                           *, n_groups, per_group):
    x = x_ref[...]
    ms = jnp.mean(x * x, axis=-1, keepdims=True)
    h = x * lax.rsqrt(ms + RMS_EPS) * g_ref[...]
    h_ref[...] = h.astype(h_ref.dtype)
    lt = lax.dot_general(wrt_ref[...], h, (((1,), (1,)), ((), ())),
                         precision=lax.Precision.HIGHEST, preferred_element_type=F32) + brt_ref[...]
    g_prob = _softmax_list([lt[i:i + 1, :] for i in range(n_groups)])
    g_top_p, g_top = _first_argmax(g_prob)
    e_sel = []
    for j in range(per_group):
        row = lt[n_groups + j:n_groups + j + 1, :]
        for grp in range(1, n_groups):
            r = n_groups + grp * per_group + j
            row = jnp.where(g_top == grp, lt[r:r + 1, :], row)
        e_sel.append(row)
    e_prob = _softmax_list(e_sel)
    p1, i1 = _first_argmax(e_prob)
    p2, i2 = _first_argmax([jnp.where(i1 == j, -1.0, e_prob[j]) for j in range(per_group)])
    denom = p1 + p2
    w1 = g_top_p * p1 / denom
    w2 = g_top_p * p2 / denom
    e1 = g_top * per_group + i1
    e2 = g_top * per_group + i2
    rows = lax.broadcasted_iota(jnp.int32, eid_ref.shape, 0)
    eid_ref[...] = jnp.where(rows == 0, e1, jnp.where(rows == 1, e2, 0))
    wt_ref[...] = jnp.where(rows == 0, w1, jnp.where(rows == 1, w2, 0.0))


def _rmsnorm_router(x2d, g, w_router_t, b_router_t, n_groups, per_group, tm=256):
    t, d = x2d.shape
    kern = functools.partial(_rmsnorm_router_kernel, n_groups=n_groups, per_group=per_group)
    return pl.pallas_call(
        kern,
        grid=(t // tm,),
        in_specs=[pl.BlockSpec((tm, d), lambda i: (i, 0)),
                  pl.BlockSpec((1, d), lambda i: (0, 0)),
                  pl.BlockSpec((ROUTER_ROWS, d), lambda i: (0, 0)),
                  pl.BlockSpec((ROUTER_ROWS, 1), lambda i: (0, 0))],
        out_specs=[pl.BlockSpec((tm, d), lambda i: (i, 0)),
                   pl.BlockSpec((SUBLANES, tm), lambda i: (0, i)),
                   pl.BlockSpec((SUBLANES, tm), lambda i: (0, i))],
        out_shape=[jax.ShapeDtypeStruct((t, d), F32),
                   jax.ShapeDtypeStruct((SUBLANES, t), jnp.int32),
                   jax.ShapeDtypeStruct((SUBLANES, t), F32)],
        compiler_params=_params(("parallel",)),
        name="rmsnorm_router",
    )(x2d, g.reshape(1, d), w_router_t, b_router_t)


def _matmul_kernel(a_ref, b_ref, o_ref, bb_ref):
    @pl.when(pl.program_id(1) == 0)
    def _():
        bb_ref[...] = b_ref[...].astype(BF16)

    o_ref[...] = jnp.dot(a_ref[...], bb_ref[...], preferred_element_type=F32).astype(o_ref.dtype)


def _matmul(a, b, layer, out_dtype, col0=0, n=None, tm=512, tn=512):
    m, k = a.shape
    n = b.shape[2] if n is None else n
    tn = min(tn, n)
    tm = min(tm, m)
    assert col0 % tn == 0 and n % tn == 0 and m % tm == 0
    cb0 = col0 // tn
    return pl.pallas_call(
        _matmul_kernel,
        grid=(n // tn, m // tm),
        in_specs=[pl.BlockSpec((tm, k), lambda j, i: (i, 0)),
                  pl.BlockSpec((None, k, tn), lambda j, i: (layer, 0, cb0 + j))],
        out_specs=pl.BlockSpec((tm, tn), lambda j, i: (i, j)),
        out_shape=jax.ShapeDtypeStruct((m, n), out_dtype),
        scratch_shapes=[pltpu.VMEM((k, tn), BF16)],
        compiler_params=_params(("parallel", "arbitrary")),
        name="proj_in",
    )(a, b)


def _out_proj_kernel(a1_ref, a2_ref, b1_ref, b2_ref, r_ref, o_ref, bb1_ref, bb2_ref):
    @pl.when(pl.program_id(1) == 0)
    def _():
        bb1_ref[...] = b1_ref[...].astype(BF16)
        bb2_ref[...] = b2_ref[...].astype(BF16)

    acc = jnp.dot(a1_ref[...], bb1_ref[...], preferred_element_type=F32)
    acc = acc + jnp.dot(a2_ref[...], bb2_ref[...], preferred_element_type=F32)
    o_ref[...] = r_ref[...] + acc


def _out_proj(a1, a2, w, layer, res, tm=512, tn=512):
    m, k1 = a1.shape
    k2 = a2.shape[1]
    n = w.shape[2]
    tn = min(tn, n)
    tm = min(tm, m)
    assert k1 == k2
    return pl.pallas_call(
        _out_proj_kernel,
        grid=(n // tn, m // tm),
        in_specs=[pl.BlockSpec((tm, k1), lambda j, i: (i, 0)),
                  pl.BlockSpec((tm, k2), lambda j, i: (i, 0)),
                  pl.BlockSpec((None, k1, tn), lambda j, i: (layer, 0, j)),
                  pl.BlockSpec((None, k2, tn), lambda j, i: (layer, 1, j)),
                  pl.BlockSpec((tm, tn), lambda j, i: (i, j))],
        out_specs=pl.BlockSpec((tm, tn), lambda j, i: (i, j)),
        out_shape=jax.ShapeDtypeStruct((m, n), F32),
        scratch_shapes=[pltpu.VMEM((k1, tn), BF16), pltpu.VMEM((k2, tn), BF16)],
        compiler_params=_params(("parallel", "arbitrary")),
        name="proj_out",
    )(a1, a2, w, w, res)


def _conv_kernel(up_ref, uc_ref, un_ref, w_ref, b_ref, o_ref, buf_ref, *, ts, scale, transpose):
    t = pl.program_id(2)
    nt = pl.num_programs(2)
    pad = SUBLANES
    buf_ref[0:pad, :] = jnp.where(t > 0, up_ref[0], 0.0)
    buf_ref[pad:pad + ts, :] = uc_ref[0]
    buf_ref[pad + ts:2 * pad + ts, :] = jnp.where(t < nt - 1, un_ref[0], 0.0)
    acc = jnp.zeros((ts, buf_ref.shape[1]), F32) + b_ref[...]
    for j in range(CONV_W):
        acc = acc + buf_ref[pl.ds(pad - CONV_W // 2 + j, ts), :] * w_ref[j:j + 1, :]
    y = acc * jax.nn.sigmoid(acc) * scale
    if transpose:
        o_ref[0] = y.T.astype(o_ref.dtype)
    else:
        o_ref[0] = y.astype(o_ref.dtype)


def _short_conv_silu(u, conv_w, conv_b, col0, width, scale, transpose, ts=512):
    bsz, s, _ = u.shape
    tc = MLSTM_DK
    ts = min(ts, s)
    cb0 = col0 // tc
    nrb = ts // SUBLANES
    last_rb = s // SUBLANES - 1
    kern = functools.partial(_conv_kernel, ts=ts, scale=scale, transpose=transpose)
    if transpose:
        out_spec = pl.BlockSpec((1, tc, ts), lambda b, c, t: (b, c, t))
        out_shape = jax.ShapeDtypeStruct((bsz, width, s), BF16)
    else:
        out_spec = pl.BlockSpec((1, ts, tc), lambda b, c, t: (b, t, c))
        out_shape = jax.ShapeDtypeStruct((bsz, s, width), BF16)
    return pl.pallas_call(
        kern,
        grid=(bsz, width // tc, s // ts),
        in_specs=[pl.BlockSpec((1, SUBLANES, tc),
                               lambda b, c, t: (b, jnp.maximum(t * nrb - 1, 0), cb0 + c)),
                  pl.BlockSpec((1, ts, tc), lambda b, c, t: (b, t, cb0 + c)),
                  pl.BlockSpec((1, SUBLANES, tc),
                               lambda b, c, t: (b, jnp.minimum((t + 1) * nrb, last_rb), cb0 + c)),
                  pl.BlockSpec((CONV_W, tc), lambda b, c, t: (0, cb0 + c)),
                  pl.BlockSpec((1, tc), lambda b, c, t: (0, cb0 + c))],
        out_specs=out_spec,
        out_shape=out_shape,
        scratch_shapes=[pltpu.VMEM((ts + 2 * SUBLANES, tc), F32)],
        compiler_params=_params(("parallel", "parallel", "parallel")),
        name="conv_silu_t" if transpose else "conv_silu",
    )(u, u, u, conv_w, conv_b.reshape(1, -1))


def _log_sigmoid(x):
    return jnp.minimum(x, 0.0) - jnp.log1p(jnp.exp(-jnp.abs(x)))


def _mlstm_kernel(bias_ref, q_ref, kt_ref, v_ref, g_ref, *rest, chunk, nheads, reverse, final):
    if final:
        hprev_ref, og_ref, hg_ref, o_ref, ct_ref, m_ref, vext_ref = rest
    else:
        o_ref, ct_ref, m_ref, vext_ref = rest
    h_idx = pl.program_id(1)
    c_idx = pl.program_id(2)
    dv = MLSTM_DV
    ln = chunk

    @pl.when(c_idx == 0)
    def _():
        ct_ref[...] = jnp.zeros_like(ct_ref)
        m_ref[...] = jnp.zeros_like(m_ref)

    gi, gf = (2, 3) if reverse else (0, 1)
    i_row = g_ref[gi] + bias_ref[gi * nheads + h_idx]
    f_row = _log_sigmoid(g_ref[gf] + bias_ref[gf * nheads + h_idx])

    row = lax.broadcasted_iota(jnp.int32, (ln, ln), 0)
    col = lax.broadcasted_iota(jnp.int32, (ln, ln), 1)
    cum_mask = (row >= col) if reverse else (row <= col)
    b_row = jnp.dot(jnp.broadcast_to(f_row, (SUBLANES, ln)), cum_mask.astype(F32),
                    precision=lax.Precision.HIGHEST, preferred_element_type=F32)[0:1, :]
    total = jnp.sum(f_row, axis=1, keepdims=True)
    b_col = jnp.sum(jnp.where(row == col, b_row, 0.0), axis=1, keepdims=True)

    m0 = m_ref[0:1, 0:1]
    tri = (col >= row) if reverse else (col <= row)
    d_log = jnp.where(tri, b_col - b_row + i_row, MASK_VALUE)
    a_col = b_col + m0
    m_t = jnp.maximum(a_col, jnp.max(d_log, axis=1, keepdims=True))
    w_intra = jnp.exp(d_log - m_t)
    w_inter = jnp.exp(a_col - m_t)

    q = q_ref[0]
    kt = kt_ref[0]
    vext_ref[:, 0:dv] = v_ref[0]
    vext_ref[:, dv:] = jnp.ones((ln, LANES), vext_ref.dtype)
    vext = vext_ref[...]

    s_qk = jnp.dot(q, kt, preferred_element_type=F32) * w_intra
    intra = jnp.dot(s_qk.astype(BF16), vext, preferred_element_type=F32)
    ct = ct_ref[...]
    inter = jnp.dot(q, ct.astype(BF16), preferred_element_type=F32)
    tot = intra + w_inter * inter
    den = tot[:, dv:dv + 1]
    h = tot[:, 0:dv] / jnp.maximum(jnp.abs(den), jnp.exp(-m_t))

    g_row = total - b_row + i_row
    m_new = jnp.maximum(total + m0, jnp.max(g_row, axis=1, keepdims=True))
    decay = jnp.exp(total + m0 - m_new)
    w_add = jnp.exp(g_row - m_new)
    ktw = (kt.astype(F32) * w_add).astype(BF16)
    ct_ref[...] = decay * ct + jnp.dot(ktw, vext, preferred_element_type=F32)
    m_ref[...] = jnp.broadcast_to(m_new, m_ref.shape)

    if final:
        hs = h + hprev_ref[0]
        hn = hs * lax.rsqrt(jnp.mean(hs * hs, axis=-1, keepdims=True) + RMS_EPS)
        o_ref[0] = (hn * hg_ref[...] * jax.nn.sigmoid(og_ref[0])).astype(o_ref.dtype)
    else:
        o_ref[0] = h


def _mlstm_scan(bias, q, kt, v, gates_t, reverse, hprev=None, ogate=None, head_g=None):
    bsz, s, _ = q.shape
    nheads = v.shape[2] // MLSTM_DV
    ln = min(MLSTM_CHUNK, s)
    nc = s // ln
    final = hprev is not None

    def cidx(c):
        return (nc - 1 - c) if reverse else c

    in_specs = [pl.BlockSpec(memory_space=pltpu.SMEM),
                pl.BlockSpec((1, ln, MLSTM_DK), lambda b, h, c: (b, cidx(c), h)),
                pl.BlockSpec((1, MLSTM_DK, ln), lambda b, h, c: (b, h, cidx(c))),
                pl.BlockSpec((1, ln, MLSTM_DV), lambda b, h, c: (b, cidx(c), h)),
                pl.BlockSpec((None, 4, None, 1, ln), lambda b, h, c: (b, 0, h, 0, cidx(c)))]
    args = [bias, q, kt, v, gates_t]
    if final:
        in_specs += [pl.BlockSpec((1, ln, MLSTM_DV), lambda b, h, c: (b, cidx(c), h)),
                     pl.BlockSpec((1, ln, MLSTM_DV), lambda b, h, c: (b, cidx(c), h)),
                     pl.BlockSpec((1, MLSTM_DV), lambda b, h, c: (0, h))]
        args += [hprev, ogate, head_g]
    kern = functools.partial(_mlstm_kernel, chunk=ln, nheads=nheads, reverse=reverse, final=final)
    return pl.pallas_call(
        kern,
        grid=(bsz, nheads, nc),
        in_specs=in_specs,
        out_specs=pl.BlockSpec((1, ln, MLSTM_DV), lambda b, h, c: (b, cidx(c), h)),
        out_shape=jax.ShapeDtypeStruct((bsz, s, nheads * MLSTM_DV), BF16 if final else F32),
        scratch_shapes=[pltpu.VMEM((MLSTM_DK, MLSTM_DV + LANES), F32),
                        pltpu.VMEM((SUBLANES, LANES), F32),
                        pltpu.VMEM((ln, MLSTM_DV + LANES), BF16)],
        compiler_params=_params(("parallel", "parallel", "arbitrary")),
        name="mlstm_fwd" if final else "mlstm_bwd",
    )(*args)


ATTN_TILE = 1024
ATTN_HALO = ATTN_HALF * 16
ATTN_HEADS_PER_STEP = 2
ATTN_SUB = 128


def _band_bias(nq, nk, slope_d):
    qi = lax.broadcasted_iota(jnp.int32, (nq, nk), 0)
    kj = lax.broadcasted_iota(jnp.int32, (nq, nk), 1)
    rel = jnp.abs(kj - ATTN_HALF - qi)
    return jnp.where(rel <= ATTN_HALF, -slope_d * rel.astype(F32), MASK_VALUE)


def _band_scores(qs, ks, vs, bias, key_lo, key_hi):
    s = lax.dot_general(qs.astype(BF16), ks.astype(BF16), (((1,), (1,)), ((), ())),
                        preferred_element_type=F32) + bias
    if key_lo is not None or key_hi is not None:
        kj = lax.broadcasted_iota(jnp.int32, s.shape, 1)
        inside = None
        if key_lo is not None:
            inside = kj >= key_lo
        if key_hi is not None:
            inside = (kj < key_hi) if inside is None else (inside & (kj < key_hi))
        s = jnp.where(inside, s, MASK_VALUE)
    m = jnp.max(s, axis=1, keepdims=True)
    p = jnp.exp(s - m)
    l = jnp.sum(p, axis=1, keepdims=True)
    o = jnp.dot(p.astype(BF16), vs.astype(BF16), preferred_element_type=F32) / l
    return o, m + jnp.log(l)


def _attn_kernel(slopes_ref, q_ref, kp_ref, kc_ref, kn_ref, vp_ref, vc_ref, vn_ref, g_ref,
                 y_ref, qbuf, kwin, vwin, acc_ref, lse_ref, *, tq):
    t = pl.program_id(1)
    cblk = pl.program_id(2)
    halo = ATTN_HALO
    half = ATTN_HALF
    dh = ATTN_HEAD_DIM
    first_tile = t == 0
    last_tile = t == pl.num_programs(1) - 1

    for hh in range(ATTN_HEADS_PER_STEP):
        sl = slice(hh * dh, (hh + 1) * dh)
        slope = slopes_ref[cblk * ATTN_HEADS_PER_STEP + hh]
        qbuf[...] = q_ref[0, :, sl] * (ATTN_HEAD_DIM ** -0.5)
        kwin[0:halo, :] = kp_ref[0, :, sl]
        kwin[halo:halo + tq, :] = kc_ref[0, :, sl]
        kwin[halo + tq:, :] = kn_ref[0, :, sl]
        vwin[0:halo, :] = vp_ref[0, :, sl]
        vwin[halo:halo + tq, :] = vc_ref[0, :, sl]
        vwin[halo + tq:, :] = vn_ref[0, :, sl]

        for p_idx, (_, dil) in enumerate(DILATED_PATTERNS):
            n_class = tq // dil
            nsub = min(ATTN_SUB, n_class)
            nk = nsub + 2 * half
            nblk = n_class // nsub
            bias = _band_bias(nsub, nk, slope * dil)
            key_lo = jnp.where(first_tile, half, 0)
            key_hi = jnp.where(last_tile, nsub + half, nk)
            for r in range(dil):
                for j in range(nblk):
                    q0 = r + dil * (j * nsub)
                    k0 = halo + r + dil * (j * nsub - half)
                    if dil == 1:
                        rows_q, rows_k = pl.ds(q0, nsub), pl.ds(k0, nk)
                    else:
                        rows_q = pl.ds(q0, nsub, stride=dil)
                        rows_k = pl.ds(k0, nk, stride=dil)
                    o, lse = _band_scores(qbuf[rows_q, :], kwin[rows_k, :], vwin[rows_k, :], bias,
                                          key_lo if j == 0 else None,
                                          key_hi if j == nblk - 1 else None)
                    acc_ref[p_idx, rows_q, :] = o
                    lse_ref[p_idx, rows_q, :] = jnp.broadcast_to(lse, (nsub, dh))

        l1, l2, l3 = lse_ref[0], lse_ref[1], lse_ref[2]
        mx = jnp.maximum(jnp.maximum(l1, l2), l3)
        e1, e2, e3 = jnp.exp(l1 - mx), jnp.exp(l2 - mx), jnp.exp(l3 - mx)
        o = (e1 * acc_ref[0] + e2 * acc_ref[1] + e3 * acc_ref[2]) / (e1 + e2 + e3)
        on = o * lax.rsqrt(jnp.mean(o * o, axis=-1, keepdims=True) + RMS_EPS)
        y_ref[0, :, sl] = (on * g_ref[:, sl]).astype(y_ref.dtype)


def _dilated_attention(slopes, q, k, v, head_g):
    bsz, s, width = q.shape
    tq = min(ATTN_TILE, s)
    halo = ATTN_HALO
    assert tq % halo == 0 and s % tq == 0
    hb = ATTN_HEADS_PER_STEP * ATTN_HEAD_DIM
    nhb = tq // halo
    last_hb = s // halo - 1
    cur = pl.BlockSpec((1, tq, hb), lambda b, t, c, sl: (b, t, c))
    prev = pl.BlockSpec((1, halo, hb), lambda b, t, c, sl: (b, jnp.maximum(t * nhb - 1, 0), c))
    nxt = pl.BlockSpec((1, halo, hb),
                       lambda b, t, c, sl: (b, jnp.minimum((t + 1) * nhb, last_hb), c))
    kern = functools.partial(_attn_kernel, tq=tq)
    dh = ATTN_HEAD_DIM
    return pl.pallas_call(
        kern,
        grid_spec=pltpu.PrefetchScalarGridSpec(
            num_scalar_prefetch=1,
            grid=(bsz, s // tq, width // hb),
            in_specs=[cur, prev, cur, nxt, prev, cur, nxt,
                      pl.BlockSpec((1, hb), lambda b, t, c, sl: (0, c))],
            out_specs=pl.BlockSpec((1, tq, hb), lambda b, t, c, sl: (b, t, c)),
            scratch_shapes=[pltpu.VMEM((tq, dh), F32),
                            pltpu.VMEM((tq + 2 * halo, dh), F32),
                            pltpu.VMEM((tq + 2 * halo, dh), F32),
                            pltpu.VMEM((len(DILATED_PATTERNS), tq, dh), F32),
                            pltpu.VMEM((len(DILATED_PATTERNS), tq, dh), F32)]),
        out_shape=jax.ShapeDtypeStruct((bsz, s, width), BF16),
        compiler_params=_params(("parallel", "parallel", "parallel")),
        name="dilated_attn",
    )(slopes, q, k, k, k, v, v, v, head_g.reshape(1, width))


def _start_row_gather(idx_ref, src_hbm, dst, sem, nrows):
    def body(r, carry):
        pltpu.make_async_copy(src_hbm.at[pl.ds(idx_ref[0, 0, r], 1), :],
                              dst.at[pl.ds(r, 1), :], sem).start()
        return carry

    lax.fori_loop(0, nrows, body, 0, unroll=8)


def _wait_row_gather(src_hbm, dst, sem, nrows):
    pltpu.make_async_copy(src_hbm.at[pl.ds(0, nrows), :], dst, sem).wait()


def _expert_up_kernel(be_ref, nb_ref, idx_ref, idx_next_ref, h_hbm, wg_ref, wu_ref, o_ref,
                      xbuf, sem):
    i = pl.program_id(0)
    n_used = nb_ref[0]
    slot = i % 2
    blk = xbuf.shape[1]

    @pl.when(i == 0)
    def _():
        _start_row_gather(idx_ref, h_hbm, xbuf.at[0], sem.at[0], blk)

    @pl.when(i + 1 < n_used)
    def _():
        _start_row_gather(idx_next_ref, h_hbm, xbuf.at[1 - slot], sem.at[1 - slot], blk)

    @pl.when(i < n_used)
    def _():
        _wait_row_gather(h_hbm, xbuf.at[slot], sem.at[slot], blk)
        x = xbuf[slot].astype(BF16)
        g = jnp.dot(x, wg_ref[...], preferred_element_type=F32)
        u = jnp.dot(x, wu_ref[...], preferred_element_type=F32)
        o_ref[...] = (g * jax.nn.sigmoid(g) * u).astype(o_ref.dtype)

    @pl.when(i >= n_used)
    def _():
        o_ref[...] = jnp.zeros_like(o_ref)


def _expert_down_kernel(be_ref, nb_ref, h_ref, wd_ref, o_ref):
    i = pl.program_id(0)

    @pl.when(i < nb_ref[0])
    def _():
        o_ref[...] = jnp.dot(h_ref[...], wd_ref[...], preferred_element_type=F32)

    @pl.when(i >= nb_ref[0])
    def _():
        o_ref[...] = jnp.zeros_like(o_ref)


def _expert_ffn(h, slot_tok, block_e, n_used, w_gate, w_up, w_down, layer):
    d = h.shape[1]
    p = slot_tok.shape[0]
    ff = w_gate.shape[3]
    blk = EXPERT_BLOCK
    nblk = p // blk
    idx = slot_tok.reshape(nblk, 1, blk)
    h1 = pl.pallas_call(
        _expert_up_kernel,
        grid_spec=pltpu.PrefetchScalarGridSpec(
            num_scalar_prefetch=2,
            grid=(nblk,),
            in_specs=[pl.BlockSpec((1, 1, blk), lambda i, be, nb: (i, 0, 0),
                                   memory_space=pltpu.SMEM),
                      pl.BlockSpec((1, 1, blk), lambda i, be, nb: (jnp.minimum(i + 1, nblk - 1), 0, 0),
                                   memory_space=pltpu.SMEM),
                      pl.BlockSpec(memory_space=pl.ANY),
                      pl.BlockSpec((None, None, d, ff), lambda i, be, nb: (layer, be[i], 0, 0)),
                      pl.BlockSpec((None, None, d, ff), lambda i, be, nb: (layer, be[i], 0, 0))],
            out_specs=pl.BlockSpec((blk, ff), lambda i, be, nb: (i, 0)),
            scratch_shapes=[pltpu.VMEM((2, blk, d), F32), pltpu.SemaphoreType.DMA((2,))]),
        out_shape=jax.ShapeDtypeStruct((p, ff), BF16),
        compiler_params=_params(("arbitrary",)),
        name="expert_up",
    )(block_e, n_used, idx, idx, h, w_gate, w_up)
    return pl.pallas_call(
        _expert_down_kernel,
        grid_spec=pltpu.PrefetchScalarGridSpec(
            num_scalar_prefetch=2,
            grid=(nblk,),
            in_specs=[pl.BlockSpec((blk, ff), lambda i, be, nb: (i, 0)),
                      pl.BlockSpec((None, None, ff, d), lambda i, be, nb: (layer, be[i], 0, 0))],
            out_specs=pl.BlockSpec((blk, d), lambda i, be, nb: (i, 0))),
        out_shape=jax.ShapeDtypeStruct((p, d), F32),
        compiler_params=_params(("arbitrary",)),
        name="expert_down",
    )(block_e, n_used, h1, w_down)


def _combine_kernel(p0_ref, p1_ref, p0n_ref, p1n_ref, x_ref, w0_ref, w1_ref, g_ref, y_hbm,
                    *rest, emit_x):
    if emit_x:
        xo_ref, ho_ref, buf, sem = rest
    else:
        ho_ref, buf, sem = rest
    i = pl.program_id(0)
    nt = pl.num_programs(0)
    slot = i % 2
    tm = buf.shape[2]

    @pl.when(i == 0)
    def _():
        _start_row_gather(p0_ref, y_hbm, buf.at[0, 0], sem.at[0], tm)
        _start_row_gather(p1_ref, y_hbm, buf.at[0, 1], sem.at[0], tm)

    @pl.when(i + 1 < nt)
    def _():
        _start_row_gather(p0n_ref, y_hbm, buf.at[1 - slot, 0], sem.at[1 - slot], tm)
        _start_row_gather(p1n_ref, y_hbm, buf.at[1 - slot, 1], sem.at[1 - slot], tm)

    _wait_row_gather(y_hbm, buf.at[slot, 0], sem.at[slot], tm)
    _wait_row_gather(y_hbm, buf.at[slot, 1], sem.at[slot], tm)
    x = x_ref[...] + w0_ref[...] * buf[slot, 0] + w1_ref[...] * buf[slot, 1]
    if emit_x:
        xo_ref[...] = x
    ms = jnp.mean(x * x, axis=-1, keepdims=True)
    ho_ref[...] = (x * lax.rsqrt(ms + RMS_EPS) * g_ref[...]).astype(ho_ref.dtype)


def _combine_norm(x2d, y, pos, wts, g, emit_x, tm=256):
    t, d = x2d.shape
    nt = t // tm
    pidx = pos.reshape(EXPERT_TOPK, nt, 1, tm)
    wcol = wts.reshape(EXPERT_TOPK, t, 1)
    cur = lambda i: (i, 0, 0)
    nxt = lambda i: (jnp.minimum(i + 1, nt - 1), 0, 0)
    ispec = lambda f: pl.BlockSpec((1, 1, tm), f, memory_space=pltpu.SMEM)
    row = pl.BlockSpec((tm, d), lambda i: (i, 0))
    wspec = pl.BlockSpec((tm, 1), lambda i: (i, 0))
    if emit_x:
        out_specs = [row, row]
        out_shape = [jax.ShapeDtypeStruct((t, d), F32), jax.ShapeDtypeStruct((t, d), BF16)]
    else:
        out_specs = row
        out_shape = jax.ShapeDtypeStruct((t, d), F32)
    return pl.pallas_call(
        functools.partial(_combine_kernel, emit_x=emit_x),
        grid=(nt,),
        in_specs=[ispec(cur), ispec(cur), ispec(nxt), ispec(nxt), row, wspec, wspec,
                  pl.BlockSpec((1, d), lambda i: (0, 0)),
                  pl.BlockSpec(memory_space=pl.ANY)],
        out_specs=out_specs,
        out_shape=out_shape,
        scratch_shapes=[pltpu.VMEM((2, EXPERT_TOPK, tm, d), F32), pltpu.SemaphoreType.DMA((2,))],
        compiler_params=_params(("arbitrary",)),
        name="moe_combine_norm",
    )(pidx[0], pidx[1], pidx[0], pidx[1], x2d, wcol[0], wcol[1], g.reshape(1, d), y)


def _route(eid, n_experts):
    t = eid.shape[1]
    a = EXPERT_TOPK * t
    flat_e = eid.reshape(a)
    onehot = (flat_e[:, None] == jnp.arange(n_experts, dtype=jnp.int32)[None, :]).astype(jnp.int32)
    csum = jnp.cumsum(onehot, axis=0)
    counts = csum[-1]
    blocks = (counts + EXPERT_BLOCK - 1) // EXPERT_BLOCK
    blk_end = jnp.cumsum(blocks)
    pad_start = (blk_end - blocks) * EXPERT_BLOCK
    dest = jnp.sum(onehot * (csum - 1 + pad_start[None, :]), axis=1)
    n_blocks = -(-a // EXPERT_BLOCK) + n_experts
    p = n_blocks * EXPERT_BLOCK
    tok = jnp.arange(a, dtype=jnp.int32) % t
    slot_tok = jnp.zeros((p,), jnp.int32).at[dest].set(tok)
    block_e = jnp.sum((blk_end[None, :] <= jnp.arange(n_blocks, dtype=jnp.int32)[:, None])
                      .astype(jnp.int32), axis=1)
    block_e = jnp.minimum(block_e, n_experts - 1)
    n_used = blk_end[-1:].astype(jnp.int32)
    return slot_tok, block_e, n_used, dest.reshape(EXPERT_TOPK, t)


def kernel(x, norm1_g, w_in, b_gate, conv_w, conv_b, head_norm_g, w_out, norm2_g, w_router_group,
           b_router_group, w_router_expert, b_router_expert, w_gate, w_up, w_down, final_norm_g):
    depth = norm1_g.shape[0]
    bsz, s, d = x.shape
    t = bsz * s
    mw = d // 2
    nmh = mw // MLSTM_DV
    qkw = nmh * MLSTM_DK
    aw = d - mw
    nah = aw // ATTN_HEAD_DIM
    ngate = 4 * nmh
    n_groups = w_router_group.shape[2]
    n_experts = w_router_expert.shape[2]

    c_v = 2 * qkw
    c_o = c_v + mw
    c_g = c_o + mw
    c_a = c_g + ngate
    w_gates = jnp.pad(w_in[:, :, c_g:c_a], ((0, 0), (0, 0), (0, LANES - ngate)))
    w_attn = w_in[:, :, c_a:]
    npad = ROUTER_ROWS - n_groups - n_experts
    w_router_t = jnp.pad(jnp.concatenate([w_router_group, w_router_expert], axis=2)
                         .transpose(0, 2, 1), ((0, 0), (0, npad), (0, 0)))
    b_router_t = jnp.pad(jnp.concatenate([b_router_group, b_router_expert], axis=1),
                         ((0, 0), (0, npad))).reshape(depth, ROUTER_ROWS, 1)
    wg_b, wu_b, wd_b = w_gate.astype(BF16), w_up.astype(BF16), w_down.astype(BF16)
    slopes = jnp.exp2(-8.0 * jnp.arange(1, nah + 1, dtype=F32) / nah)

    x2d = x.reshape(t, d)
    h = _rmsnorm(x2d, norm1_g[0], BF16)
    for l in range(depth):
        head_g = head_norm_g[l]
        qk_m = _matmul(h, w_in, l, F32, col0=0, n=2 * qkw).reshape(bsz, s, 2 * qkw)
        v_m = _matmul(h, w_in, l, BF16, col0=c_v, n=mw).reshape(bsz, s, mw)
        o_m = _matmul(h, w_in, l, F32, col0=c_o, n=mw).reshape(bsz, s, mw)
        gates = _matmul(h, w_gates, l, F32)[:, :ngate]
        q_a = _matmul(h, w_attn, l, F32, col0=0, n=aw).reshape(bsz, s, aw)
        k_a = _matmul(h, w_attn, l, F32, col0=aw, n=aw).reshape(bsz, s, aw)
        v_a = _matmul(h, w_attn, l, F32, col0=2 * aw, n=aw).reshape(bsz, s, aw)

        q_m = _short_conv_silu(qk_m, conv_w[l], conv_b[l], 0, qkw, MLSTM_DK ** -0.5,
                               transpose=False)
        kt_m = _short_conv_silu(qk_m, conv_w[l], conv_b[l], qkw, qkw, 1.0, transpose=True)
        gates_t = gates.reshape(bsz, s, 4, nmh).transpose(0, 2, 3, 1).reshape(bsz, 4, nmh, 1, s)
        h_bwd = _mlstm_scan(b_gate[l], q_m, kt_m, v_m, gates_t, reverse=True)
        y_m = _mlstm_scan(b_gate[l], q_m, kt_m, v_m, gates_t, reverse=False, hprev=h_bwd,
                          ogate=o_m, head_g=head_g[:mw].reshape(1, mw))

        y_a = _dilated_attention(slopes, q_a, k_a, v_a, head_g[mw:])

        x2d = _out_proj(y_m.reshape(t, mw), y_a.reshape(t, aw), w_out, l, x2d)

        h2, eid, wts = _rmsnorm_router(x2d, norm2_g[l], w_router_t[l], b_router_t[l],
                                       n_groups, n_experts // n_groups)
        slot_tok, block_e, n_used, pos = _route(eid[:EXPERT_TOPK], n_experts)
        yb = _expert_ffn(h2, slot_tok, block_e, n_used, wg_b, wu_b, wd_b, l)
        if l + 1 < depth:
            x2d, h = _combine_norm(x2d, yb, pos, wts[:EXPERT_TOPK], norm1_g[l + 1], emit_x=True)
        else:
            out = _combine_norm(x2d, yb, pos, wts[:EXPERT_TOPK], final_norm_g, emit_x=False)
    return out.reshape(bsz, s, d)
```

```python
import functools

import jax
import jax.numpy as jnp
from jax import lax
from jax.experimental import pallas as pl
from jax.experimental.pallas import tpu as pltpu

MLSTM_DV = 512
MLSTM_DK = 256
ATTN_HEAD_DIM = 128
CONV_W = 5
DILATED_PATTERNS = ((128, 1), (512, 4), (2048, 16))
ATTN_HALF = 64
EXPERT_TOPK = 2
RMS_EPS = 1e-6
MASK_VALUE = -1e30

LANES = 128
SUBLANES = 8
VMEM_LIMIT_BYTES = 56 * 1024 * 1024
PROJ_VMEM_LIMIT_BYTES = 60 * 1024 * 1024

MLSTM_CHUNK = 256
EXPERT_BLOCK = 256
ROUTER_ROWS = 32

BF16 = jnp.bfloat16
F32 = jnp.float32


def _params(semantics):
    return pltpu.CompilerParams(dimension_semantics=semantics, vmem_limit_bytes=VMEM_LIMIT_BYTES)


def _rmsnorm_kernel(x_ref, g_ref, o_ref):
    x = x_ref[...]
    ms = jnp.mean(x * x, axis=-1, keepdims=True)
    o_ref[...] = (x * lax.rsqrt(ms + RMS_EPS) * g_ref[...]).astype(o_ref.dtype)


def _rmsnorm(x2d, g, out_dtype, tm=256):
    t, d = x2d.shape
    return pl.pallas_call(
        _rmsnorm_kernel,
        grid=(t // tm,),
        in_specs=[pl.BlockSpec((tm, d), lambda i: (i, 0)),
                  pl.BlockSpec((1, d), lambda i: (0, 0))],
        out_specs=pl.BlockSpec((tm, d), lambda i: (i, 0)),
        out_shape=jax.ShapeDtypeStruct((t, d), out_dtype),
        compiler_params=_params(("parallel",)),
        name="rmsnorm",
    )(x2d, g.reshape(1, d))


def _first_argmax(vals):
    best = vals[0]
    idx = jnp.zeros(best.shape, jnp.int32)
    for i in range(1, len(vals)):
        gt = vals[i] > best
        best = jnp.where(gt, vals[i], best)
        idx = jnp.where(gt, i, idx)
    return best, idx


def _softmax_list(vals):
    mx = functools.reduce(jnp.maximum, vals)
    es = [jnp.exp(v - mx) for v in vals]
    tot = functools.reduce(lambda a, b: a + b, es)
    return [e / tot for e in es]


def _rmsnorm_router_kernel(x_ref, g_ref, wrt_ref, brt_ref, h_ref, eid_ref, wt_ref,
                           *, n_groups, per_group):
    x = x_ref[...]
    ms = jnp.mean(x * x, axis=-1, keepdims=True)
    h = x * lax.rsqrt(ms + RMS_EPS) * g_ref[...]
    h_ref[...] = h.astype(h_ref.dtype)
    def split(v):
        hi = v.astype(BF16)
        return hi, (v - hi.astype(F32)).astype(BF16)

    def dot_nt(a, b):
        return lax.dot_general(a, b, (((1,), (1,)), ((), ())), preferred_element_type=F32)

    w_hi, w_lo = split(wrt_ref[...])
    h_hi, h_lo = split(h)
    lt = dot_nt(w_hi, h_hi) + dot_nt(w_lo, h_hi) + dot_nt(w_hi, h_lo) + brt_ref[...]
    g_prob = _softmax_list([lt[i:i + 1, :] for i in range(n_groups)])
    g_top_p, g_top = _first_argmax(g_prob)
    e_sel = []
    for j in range(per_group):
        row = lt[n_groups + j:n_groups + j + 1, :]
        for grp in range(1, n_groups):
            r = n_groups + grp * per_group + j
            row = jnp.where(g_top == grp, lt[r:r + 1, :], row)
        e_sel.append(row)
    e_prob = _softmax_list(e_sel)
    p1, i1 = _first_argmax(e_prob)
    p2, i2 = _first_argmax([jnp.where(i1 == j, -1.0, e_prob[j]) for j in range(per_group)])
    denom = p1 + p2
    w1 = g_top_p * p1 / denom
    w2 = g_top_p * p2 / denom
    e1 = g_top * per_group + i1
    e2 = g_top * per_group + i2
    rows = lax.broadcasted_iota(jnp.int32, eid_ref.shape, 0)
    eid_ref[...] = jnp.where(rows == 0, e1, jnp.where(rows == 1, e2, 0))
    wt_ref[...] = jnp.where(rows == 0, w1, jnp.where(rows == 1, w2, 0.0))


def _rmsnorm_router(x2d, g, w_router_t, b_router_t, n_groups, per_group, tm=256):
    t, d = x2d.shape
    kern = functools.partial(_rmsnorm_router_kernel, n_groups=n_groups, per_group=per_group)
    return pl.pallas_call(
        kern,
        grid=(t // tm,),
        in_specs=[pl.BlockSpec((tm, d), lambda i: (i, 0)),
                  pl.BlockSpec((1, d), lambda i: (0, 0)),
                  pl.BlockSpec((ROUTER_ROWS, d), lambda i: (0, 0)),
                  pl.BlockSpec((ROUTER_ROWS, 1), lambda i: (0, 0))],
        out_specs=[pl.BlockSpec((tm, d), lambda i: (i, 0)),
                   pl.BlockSpec((SUBLANES, tm), lambda i: (0, i)),
                   pl.BlockSpec((SUBLANES, tm), lambda i: (0, i))],
        out_shape=[jax.ShapeDtypeStruct((t, d), F32),
                   jax.ShapeDtypeStruct((SUBLANES, t), jnp.int32),
                   jax.ShapeDtypeStruct((SUBLANES, t), F32)],
        compiler_params=_params(("parallel",)),
        name="rmsnorm_router",
    )(x2d, g.reshape(1, d), w_router_t, b_router_t)


WEIGHT_CAST_ROWS = 256


def _matmul_kernel(a_ref, b_ref, *rest, lane_shift):
    if lane_shift:
        bx_ref, o_ref, bb_ref = rest
    else:
        o_ref, bb_ref = rest
    k, tn = bb_ref.shape

    @pl.when(pl.program_id(1) == 0)
    def _():
        def body(c, carry):
            rows = pl.ds(pl.multiple_of(c * WEIGHT_CAST_ROWS, WEIGHT_CAST_ROWS), WEIGHT_CAST_ROWS)
            w = b_ref[rows, :]
            if lane_shift:
                w = jnp.concatenate([w, bx_ref[rows, :]], axis=1)[:, lane_shift:lane_shift + tn]
            bb_ref[rows, :] = w.astype(BF16)
            return carry

        lax.fori_loop(0, k // WEIGHT_CAST_ROWS, body, 0)

    o_ref[...] = jnp.dot(a_ref[...], bb_ref[...], preferred_element_type=F32).astype(o_ref.dtype)


def _matmul(a, b, layer, out_dtype, col0, n, lane_shift=0, tm=512, tn=1024):
    m, k = a.shape
    tn = min(tn, n)
    tm = min(tm, m)
    assert col0 % tn == 0 and n % tn == 0 and m % tm == 0 and k % WEIGHT_CAST_ROWS == 0
    cb0 = col0 // tn
    in_specs = [pl.BlockSpec((tm, k), lambda j, i: (i, 0)),
                pl.BlockSpec((None, k, tn), lambda j, i: (layer, 0, cb0 + j))]
    args = [a, b]
    if lane_shift:
        lb = tn // LANES
        in_specs.append(pl.BlockSpec((None, k, LANES), lambda j, i: (layer, 0, (cb0 + j + 1) * lb)))
        args.append(b)
    return pl.pallas_call(
        functools.partial(_matmul_kernel, lane_shift=lane_shift),
        grid=(n // tn, m // tm),
        in_specs=in_specs,
        out_specs=pl.BlockSpec((tm, tn), lambda j, i: (i, j)),
        out_shape=jax.ShapeDtypeStruct((m, n), out_dtype),
        scratch_shapes=[pltpu.VMEM((k, tn), BF16)],
        compiler_params=pltpu.CompilerParams(dimension_semantics=("parallel", "arbitrary"),
                                             vmem_limit_bytes=PROJ_VMEM_LIMIT_BYTES),
        name="proj_in",
    )(*args)


def _out_proj_kernel(a1_ref, a2_ref, b1_ref, b2_ref, r_ref, o_ref, bb1_ref, bb2_ref):
    @pl.when(pl.program_id(1) == 0)
    def _():
        bb1_ref[...] = b1_ref[...].astype(BF16)
        bb2_ref[...] = b2_ref[...].astype(BF16)

    acc = jnp.dot(a1_ref[...], bb1_ref[...], preferred_element_type=F32)
    acc = acc + jnp.dot(a2_ref[...], bb2_ref[...], preferred_element_type=F32)
    o_ref[...] = r_ref[...] + acc


def _out_proj(a1, a2, w, layer, res, tm=512, tn=512):
    m, k1 = a1.shape
    k2 = a2.shape[1]
    n = w.shape[2]
    tn = min(tn, n)
    tm = min(tm, m)
    assert k1 == k2
    return pl.pallas_call(
        _out_proj_kernel,
        grid=(n // tn, m // tm),
        in_specs=[pl.BlockSpec((tm, k1), lambda j, i: (i, 0)),
                  pl.BlockSpec((tm, k2), lambda j, i: (i, 0)),
                  pl.BlockSpec((None, k1, tn), lambda j, i: (layer, 0, j)),
                  pl.BlockSpec((None, k2, tn), lambda j, i: (layer, 1, j)),
                  pl.BlockSpec((tm, tn), lambda j, i: (i, j))],
        out_specs=pl.BlockSpec((tm, tn), lambda j, i: (i, j)),
        out_shape=jax.ShapeDtypeStruct((m, n), F32),
        scratch_shapes=[pltpu.VMEM((k1, tn), BF16), pltpu.VMEM((k2, tn), BF16)],
        compiler_params=_params(("parallel", "arbitrary")),
        name="proj_out",
    )(a1, a2, w, w, res)


def _conv_kernel(up_ref, uc_ref, un_ref, w_ref, b_ref, o_ref, buf_ref, *, ts, scale, transpose):
    t = pl.program_id(2)
    nt = pl.num_programs(2)
    pad = SUBLANES
    buf_ref[0:pad, :] = jnp.where(t > 0, up_ref[0], 0.0)
    buf_ref[pad:pad + ts, :] = uc_ref[0]
    buf_ref[pad + ts:2 * pad + ts, :] = jnp.where(t < nt - 1, un_ref[0], 0.0)
    acc = jnp.zeros((ts, buf_ref.shape[1]), F32) + b_ref[...]
    for j in range(CONV_W):
        acc = acc + buf_ref[pl.ds(pad - CONV_W // 2 + j, ts), :] * w_ref[j:j + 1, :]
    y = acc * jax.nn.sigmoid(acc) * scale
    if transpose:
        o_ref[0] = y.T.astype(o_ref.dtype)
    else:
        o_ref[0] = y.astype(o_ref.dtype)


def _short_conv_silu(u, conv_w, conv_b, col0, width, scale, transpose, ts=512):
    bsz, s, _ = u.shape
    tc = MLSTM_DK
    ts = min(ts, s)
    cb0 = col0 // tc
    nrb = ts // SUBLANES
    last_rb = s // SUBLANES - 1
    kern = functools.partial(_conv_kernel, ts=ts, scale=scale, transpose=transpose)
    if transpose:
        out_spec = pl.BlockSpec((1, tc, ts), lambda b, c, t: (b, c, t))
        out_shape = jax.ShapeDtypeStruct((bsz, width, s), BF16)
    else:
        out_spec = pl.BlockSpec((1, ts, tc), lambda b, c, t: (b, t, c))
        out_shape = jax.ShapeDtypeStruct((bsz, s, width), BF16)
    return pl.pallas_call(
        kern,
        grid=(bsz, width // tc, s // ts),
        in_specs=[pl.BlockSpec((1, SUBLANES, tc),
                               lambda b, c, t: (b, jnp.maximum(t * nrb - 1, 0), cb0 + c)),
                  pl.BlockSpec((1, ts, tc), lambda b, c, t: (b, t, cb0 + c)),
                  pl.BlockSpec((1, SUBLANES, tc),
                               lambda b, c, t: (b, jnp.minimum((t + 1) * nrb, last_rb), cb0 + c)),
                  pl.BlockSpec((CONV_W, tc), lambda b, c, t: (0, cb0 + c)),
                  pl.BlockSpec((1, tc), lambda b, c, t: (0, cb0 + c))],
        out_specs=out_spec,
        out_shape=out_shape,
        scratch_shapes=[pltpu.VMEM((ts + 2 * SUBLANES, tc), F32)],
        compiler_params=_params(("parallel", "parallel", "parallel")),
        name="conv_silu_t" if transpose else "conv_silu",
    )(u, u, u, conv_w, conv_b.reshape(1, -1))


def _log_sigmoid(x):
    return jnp.minimum(x, 0.0) - jnp.log1p(jnp.exp(-jnp.abs(x)))


def _mlstm_kernel(bias_ref, q_ref, kt_ref, v_ref, g_ref, *rest, chunk, nheads, reverse, final):
    if final:
        hprev_ref, og_ref, hg_ref, o_ref, ct_ref, m_ref, vext_ref = rest
    else:
        o_ref, ct_ref, m_ref, vext_ref = rest
    c_idx = pl.program_id(1)
    dv, dk = MLSTM_DV, MLSTM_DK
    ln = chunk

    @pl.when(c_idx == 0)
    def _():
        ct_ref[...] = jnp.zeros_like(ct_ref)
        m_ref[...] = jnp.zeros_like(m_ref)

    gi, gf = (2, 3) if reverse else (0, 1)
    row = lax.broadcasted_iota(jnp.int32, (ln, ln), 0)
    col = lax.broadcasted_iota(jnp.int32, (ln, ln), 1)
    cum_mask = ((row >= col) if reverse else (row <= col)).astype(F32)
    tri = (col >= row) if reverse else (col <= row)
    diag = row == col
    ones_cols = jnp.ones((ln, LANES), vext_ref.dtype)

    for hh in range(nheads):
        i_row = g_ref[gi, hh] + bias_ref[gi * nheads + hh]
        f_row = _log_sigmoid(g_ref[gf, hh] + bias_ref[gf * nheads + hh])
        b_row = jnp.dot(jnp.broadcast_to(f_row, (SUBLANES, ln)), cum_mask,
                        precision=lax.Precision.HIGHEST, preferred_element_type=F32)[0:1, :]
        total = jnp.sum(f_row, axis=1, keepdims=True)
        b_col = jnp.sum(jnp.where(diag, b_row, 0.0), axis=1, keepdims=True)

        m0 = m_ref[hh, 0:1, 0:1]
        d_log = jnp.where(tri, b_col - b_row + i_row, MASK_VALUE)
        a_col = b_col + m0
        m_t = jnp.maximum(a_col, jnp.max(d_log, axis=1, keepdims=True))
        w_intra = jnp.exp(d_log - m_t)
        w_inter = jnp.exp(a_col - m_t)

        q = q_ref[0, :, hh * dk:(hh + 1) * dk]
        kt = kt_ref[0, hh * dk:(hh + 1) * dk, :]
        vext_ref[hh, :, 0:dv] = v_ref[0, :, hh * dv:(hh + 1) * dv]
        vext_ref[hh, :, dv:] = ones_cols
        vext = vext_ref[hh]

        s_qk = jnp.dot(q, kt, preferred_element_type=F32) * w_intra
        intra = jnp.dot(s_qk.astype(BF16), vext, preferred_element_type=F32)
        ct = ct_ref[hh]
        inter = jnp.dot(q, ct.astype(BF16), preferred_element_type=F32)
        tot = intra + w_inter * inter
        den = tot[:, dv:dv + 1]
        h = tot[:, 0:dv] / jnp.maximum(jnp.abs(den), jnp.exp(-m_t))

        g_row = total - b_row + i_row
        m_new = jnp.maximum(total + m0, jnp.max(g_row, axis=1, keepdims=True))
        decay = jnp.exp(total + m0 - m_new)
        w_add = jnp.exp(g_row - m_new)
        ktw = (kt.astype(F32) * w_add).astype(BF16)
        ct_ref[hh] = decay * ct + jnp.dot(ktw, vext, preferred_element_type=F32)
        m_ref[hh] = jnp.broadcast_to(m_new, m_ref.shape[1:])

        cols = slice(hh * dv, (hh + 1) * dv)
        if final:
            hs = h + hprev_ref[0, :, cols]
            hn = hs * lax.rsqrt(jnp.mean(hs * hs, axis=-1, keepdims=True) + RMS_EPS)
            o_ref[0, :, cols] = (hn * hg_ref[:, cols]
                                 * jax.nn.sigmoid(og_ref[0, :, cols])).astype(o_ref.dtype)
        else:
            o_ref[0, :, cols] = h


def _mlstm_scan(bias, q, kt, v, gates_t, reverse, hprev=None, ogate=None, head_g=None):
    bsz, s, _ = q.shape
    nheads = v.shape[2] // MLSTM_DV
    ln = min(MLSTM_CHUNK, s)
    nc = s // ln
    final = hprev is not None

    def cidx(c):
        return (nc - 1 - c) if reverse else c

    qkw, vw = nheads * MLSTM_DK, nheads * MLSTM_DV
    wide = pl.BlockSpec((1, ln, vw), lambda b, c: (b, cidx(c), 0))
    in_specs = [pl.BlockSpec(memory_space=pltpu.SMEM),
                pl.BlockSpec((1, ln, qkw), lambda b, c: (b, cidx(c), 0)),
                pl.BlockSpec((1, qkw, ln), lambda b, c: (b, 0, cidx(c))),
                wide,
                pl.BlockSpec((None, 4, nheads, 1, ln), lambda b, c: (b, 0, 0, 0, cidx(c)))]
    args = [bias, q, kt, v, gates_t]
    if final:
        in_specs += [wide, wide, pl.BlockSpec((1, vw), lambda b, c: (0, 0))]
        args += [hprev, ogate, head_g]
    kern = functools.partial(_mlstm_kernel, chunk=ln, nheads=nheads, reverse=reverse, final=final)
    return pl.pallas_call(
        kern,
        grid=(bsz, nc),
        in_specs=in_specs,
        out_specs=wide,
        out_shape=jax.ShapeDtypeStruct((bsz, s, vw), BF16 if final else F32),
        scratch_shapes=[pltpu.VMEM((nheads, MLSTM_DK, MLSTM_DV + LANES), F32),
                        pltpu.VMEM((nheads, SUBLANES, LANES), F32),
                        pltpu.VMEM((nheads, ln, MLSTM_DV + LANES), BF16)],
        compiler_params=_params(("parallel", "arbitrary")),
        name="mlstm_fwd" if final else "mlstm_bwd",
    )(*args)


ATTN_TILE = 1024
ATTN_HALO = ATTN_HALF * 16
ATTN_SUB = 128


def _band_bias(nq, nk, slope_d):
    qi = lax.broadcasted_iota(jnp.int32, (nq, nk), 0)
    kj = lax.broadcasted_iota(jnp.int32, (nq, nk), 1)
    rel = jnp.abs(kj - ATTN_HALF - qi)
    return jnp.where(rel <= ATTN_HALF, -slope_d * rel.astype(F32), MASK_VALUE)


def _band_scores(qs, ks, bias, key_lo, key_hi):
    s = lax.dot_general(qs.astype(BF16), ks.astype(BF16), (((1,), (1,)), ((), ())),
                        preferred_element_type=F32) + bias
    if key_lo is not None or key_hi is not None:
        kj = lax.broadcasted_iota(jnp.int32, s.shape, 1)
        inside = None
        if key_lo is not None:
            inside = kj >= key_lo
        if key_hi is not None:
            inside = (kj < key_hi) if inside is None else (inside & (kj < key_hi))
        s = jnp.where(inside, s, MASK_VALUE)
    m = jnp.max(s, axis=1, keepdims=True)
    return s, jnp.broadcast_to(m, (s.shape[0], LANES))


def _band_softmax_pv(s, m_b, vs):
    nq, nk = s.shape
    reps = [m_b] * (nk // LANES) + ([m_b[:, :nk % LANES]] if nk % LANES else [])
    m_wide = reps[0] if len(reps) == 1 else jnp.concatenate(reps, axis=1)
    p = jnp.exp(s - m_wide).astype(BF16)
    v_ext = jnp.concatenate([vs.astype(BF16), jnp.ones((nk, LANES), BF16)], axis=1)
    r = jnp.dot(p, v_ext, preferred_element_type=F32)
    l_b = r[:, ATTN_HEAD_DIM:]
    return r[:, :ATTN_HEAD_DIM] / l_b, m_b + jnp.log(l_b)


def _attn_kernel(slopes_ref, q_ref, kp_ref, kc_ref, kn_ref, vp_ref, vc_ref, vn_ref, g_ref, y_ref,
                 q4, k4, v4, acc1, lse1, acc2, lse2, acc3, lse3, ynat, sbuf, mbuf, *, tq):
    t = pl.program_id(1)
    half = ATTN_HALF
    dh = ATTN_HEAD_DIM
    halo = ATTN_HALO
    n4, h4 = tq // 4, halo // 4
    n16 = tq // 16
    sub = ATTN_SUB
    scale = ATTN_HEAD_DIM ** -0.5
    first_tile = t == 0
    last_tile = t == pl.num_programs(1) - 1
    slope = slopes_ref[pl.program_id(2)]

    for c in range(4):
        q4[c] = q_ref[0, pl.ds(c, n4, stride=4), :] * scale
        for dst, (p_ref, c_ref, n_ref) in ((k4, (kp_ref, kc_ref, kn_ref)),
                                           (v4, (vp_ref, vc_ref, vn_ref))):
            dst[c, 0:h4, :] = p_ref[0, pl.ds(c, h4, stride=4), :]
            dst[c, h4:h4 + n4, :] = c_ref[0, pl.ds(c, n4, stride=4), :]
            dst[c, h4 + n4:, :] = n_ref[0, pl.ds(c, h4, stride=4), :]

    def bounds(j, nblk, nsub):
        lo = jnp.where(first_tile, half, 0) if j == 0 else None
        hi = jnp.where(last_tile, nsub + half, nsub + 2 * half) if j == nblk - 1 else None
        return lo, hi

    def run_blocks(blocks, nsub, slope_d):
        nk = nsub + 2 * half
        bias = _band_bias(nsub, nk, slope_d)
        for i, (load_q, load_k, _, lo, hi, _) in enumerate(blocks):
            s, m_b = _band_scores(load_q(), load_k(), bias, lo, hi)
            sbuf[i, 0:nsub, 0:nk] = s
            mbuf[i, 0:nsub, :] = m_b
        for i, (_, _, load_v, _, _, store) in enumerate(blocks):
            store(*_band_softmax_pv(sbuf[i, 0:nsub, 0:nk], mbuf[i, 0:nsub, :], load_v()))

    def window(refs, r0, nk):
        p_ref, c_ref, n_ref = refs
        parts = []
        if r0 < 0:
            parts.append(p_ref[0, halo + r0:halo, :])
        parts.append(c_ref[0, max(r0, 0):min(r0 + nk, tq), :])
        if r0 + nk > tq:
            parts.append(n_ref[0, 0:r0 + nk - tq, :])
        return parts[0] if len(parts) == 1 else jnp.concatenate(parts, axis=0)

    def store_to(acc, lse, *idx):
        def store(o, lse_b):
            acc[idx] = o
            lse[idx] = lse_b
        return store

    krefs, vrefs = (kp_ref, kc_ref, kn_ref), (vp_ref, vc_ref, vn_ref)
    nk = sub + 2 * half
    nblk = tq // sub
    blocks = []
    for j in range(nblk):
        rows = slice(j * sub, (j + 1) * sub)
        blocks.append((lambda rows=rows: q_ref[0, rows, :] * scale,
                       lambda j=j: window(krefs, j * sub - half, nk),
                       lambda j=j: window(vrefs, j * sub - half, nk),
                       *bounds(j, nblk, sub), store_to(acc1, lse1, rows, slice(None))))
    run_blocks(blocks, sub, slope)

    nblk = n4 // sub
    blocks = []
    for c in range(4):
        for j in range(nblk):
            rows = slice(j * sub, (j + 1) * sub)
            krows = slice(h4 + j * sub - half, h4 + j * sub - half + nk)
            blocks.append((lambda c=c, rows=rows: q4[c, rows, :],
                           lambda c=c, krows=krows: k4[c, krows, :],
                           lambda c=c, krows=krows: v4[c, krows, :],
                           *bounds(j, nblk, sub), store_to(acc2, lse2, c, rows, slice(None))))
    run_blocks(blocks, sub, slope * 4)

    nsub = min(sub, n16)
    nk = nsub + 2 * half
    nblk = n16 // nsub
    blocks = []
    for c in range(4):
        for c16 in range(4):
            for j in range(nblk):
                rows_q = pl.ds(c16 + 4 * j * nsub, nsub, stride=4)
                rows_k = pl.ds(h4 + c16 + 4 * (j * nsub - half), nk, stride=4)
                blocks.append((lambda c=c, rows_q=rows_q: q4[c, rows_q, :],
                               lambda c=c, rows_k=rows_k: k4[c, rows_k, :],
                               lambda c=c, rows_k=rows_k: v4[c, rows_k, :],
                               *bounds(j, nblk, nsub),
                               store_to(acc3, lse3, c, rows_q, slice(None))))
    run_blocks(blocks, nsub, slope * 16)

    for c in range(4):
        rows = pl.ds(c, n4, stride=4)
        l1, l2, l3 = lse1[rows, :], lse2[c], lse3[c]
        mx = jnp.maximum(jnp.maximum(l1, l2), l3)
        e1, e2, e3 = jnp.exp(l1 - mx), jnp.exp(l2 - mx), jnp.exp(l3 - mx)
        o = (e1 * acc1[rows, :] + e2 * acc2[c] + e3 * acc3[c]) / (e1 + e2 + e3)
        on = o * lax.rsqrt(jnp.mean(o * o, axis=-1, keepdims=True) + RMS_EPS)
        ynat[rows, :] = on * g_ref[...]
    y_ref[0] = ynat[...].astype(y_ref.dtype)


def _dilated_attention(slopes, q, k, v, head_g):
    bsz, s, width = q.shape
    tq = min(ATTN_TILE, s)
    halo = ATTN_HALO
    assert tq % halo == 0 and s % tq == 0 and (tq // 16) % min(ATTN_SUB, tq // 16) == 0
    dh = ATTN_HEAD_DIM
    nhb = tq // halo
    last_hb = s // halo - 1
    n4, h4 = tq // 4, halo // 4
    cur = pl.BlockSpec((1, tq, dh), lambda b, t, c, sl: (b, t, c))
    prev = pl.BlockSpec((1, halo, dh), lambda b, t, c, sl: (b, jnp.maximum(t * nhb - 1, 0), c))
    nxt = pl.BlockSpec((1, halo, dh),
                       lambda b, t, c, sl: (b, jnp.minimum((t + 1) * nhb, last_hb), c))
    kern = functools.partial(_attn_kernel, tq=tq)
    nstage = max(tq // ATTN_SUB, 16 * ((tq // 16) // min(ATTN_SUB, tq // 16)))
    win = pltpu.VMEM((4, n4 + 2 * h4, dh), F32)
    cls = pltpu.VMEM((4, n4, dh), F32)
    nat = pltpu.VMEM((tq, dh), F32)
    return pl.pallas_call(
        kern,
        grid_spec=pltpu.PrefetchScalarGridSpec(
            num_scalar_prefetch=1,
            grid=(bsz, s // tq, width // dh),
            in_specs=[cur, prev, cur, nxt, prev, cur, nxt,
                      pl.BlockSpec((1, dh), lambda b, t, c, sl: (0, c))],
            out_specs=pl.BlockSpec((1, tq, dh), lambda b, t, c, sl: (b, t, c)),
            scratch_shapes=[cls, win, win, nat, nat, cls, cls, cls, cls, nat,
                            pltpu.VMEM((nstage, ATTN_SUB, ATTN_SUB + 2 * ATTN_HALF), F32),
                            pltpu.VMEM((nstage, ATTN_SUB, LANES), F32)]),
        out_shape=jax.ShapeDtypeStruct((bsz, s, width), BF16),
        compiler_params=_params(("parallel", "parallel", "parallel")),
        name="dilated_attn",
    )(slopes, q, k, k, k, v, v, v, head_g.reshape(1, width))


def _start_row_gather(idx_ref, src_hbm, dst, sem, nrows):
    def body(r, carry):
        pltpu.make_async_copy(src_hbm.at[pl.ds(idx_ref[0, 0, r], 1), :],
                              dst.at[pl.ds(r, 1), :], sem).start()
        return carry

    lax.fori_loop(0, nrows, body, 0, unroll=8)


def _wait_row_gather(src_hbm, dst, sem, nrows):
    pltpu.make_async_copy(src_hbm.at[pl.ds(0, nrows), :], dst, sem).wait()


def _expert_up_kernel(be_ref, nb_ref, idx_ref, idx_next_ref, h_hbm, wg_ref, wu_ref, o_ref,
                      xbuf, sem):
    i = pl.program_id(0)
    n_used = nb_ref[0]
    slot = i % 2
    blk = xbuf.shape[1]

    @pl.when(i == 0)
    def _():
        _start_row_gather(idx_ref, h_hbm, xbuf.at[0], sem.at[0], blk)

    @pl.when(i + 1 < n_used)
    def _():
        _start_row_gather(idx_next_ref, h_hbm, xbuf.at[1 - slot], sem.at[1 - slot], blk)

    @pl.when(i < n_used)
    def _():
        _wait_row_gather(h_hbm, xbuf.at[slot], sem.at[slot], blk)
        x = xbuf[slot].astype(BF16)
        g = jnp.dot(x, wg_ref[...], preferred_element_type=F32)
        u = jnp.dot(x, wu_ref[...], preferred_element_type=F32)
        o_ref[...] = (g * jax.nn.sigmoid(g) * u).astype(o_ref.dtype)

    @pl.when(i >= n_used)
    def _():
        o_ref[...] = jnp.zeros_like(o_ref)


def _expert_down_kernel(be_ref, nb_ref, h_ref, wd_ref, o_ref):
    i = pl.program_id(0)

    @pl.when(i < nb_ref[0])
    def _():
        o_ref[...] = jnp.dot(h_ref[...], wd_ref[...], preferred_element_type=F32)

    @pl.when(i >= nb_ref[0])
    def _():
        o_ref[...] = jnp.zeros_like(o_ref)


def _expert_ffn(h, slot_tok, block_e, n_used, w_gate, w_up, w_down, layer):
    d = h.shape[1]
    p = slot_tok.shape[0]
    ff = w_gate.shape[3]
    blk = EXPERT_BLOCK
    nblk = p // blk
    idx = slot_tok.reshape(nblk, 1, blk)
    h1 = pl.pallas_call(
        _expert_up_kernel,
        grid_spec=pltpu.PrefetchScalarGridSpec(
            num_scalar_prefetch=2,
            grid=(nblk,),
            in_specs=[pl.BlockSpec((1, 1, blk), lambda i, be, nb: (i, 0, 0),
                                   memory_space=pltpu.SMEM),
                      pl.BlockSpec((1, 1, blk), lambda i, be, nb: (jnp.minimum(i + 1, nblk - 1), 0, 0),
                                   memory_space=pltpu.SMEM),
                      pl.BlockSpec(memory_space=pl.ANY),
                      pl.BlockSpec((None, None, d, ff), lambda i, be, nb: (layer, be[i], 0, 0)),
                      pl.BlockSpec((None, None, d, ff), lambda i, be, nb: (layer, be[i], 0, 0))],
            out_specs=pl.BlockSpec((blk, ff), lambda i, be, nb: (i, 0)),
            scratch_shapes=[pltpu.VMEM((2, blk, d), F32), pltpu.SemaphoreType.DMA((2,))]),
        out_shape=jax.ShapeDtypeStruct((p, ff), BF16),
        compiler_params=_params(("arbitrary",)),
        name="expert_up",
    )(block_e, n_used, idx, idx, h, w_gate, w_up)
    return pl.pallas_call(
        _expert_down_kernel,
        grid_spec=pltpu.PrefetchScalarGridSpec(
            num_scalar_prefetch=2,
            grid=(nblk,),
            in_specs=[pl.BlockSpec((blk, ff), lambda i, be, nb: (i, 0)),
                      pl.BlockSpec((None, None, ff, d), lambda i, be, nb: (layer, be[i], 0, 0))],
            out_specs=pl.BlockSpec((blk, d), lambda i, be, nb: (i, 0))),
        out_shape=jax.ShapeDtypeStruct((p, d), F32),
        compiler_params=_params(("arbitrary",)),
        name="expert_down",
    )(block_e, n_used, h1, w_down)


def _combine_kernel(p0_ref, p1_ref, p0n_ref, p1n_ref, x_ref, w0_ref, w1_ref, g_ref, y_hbm,
                    *rest, emit_x):
    if emit_x:
        xo_ref, ho_ref, buf, sem = rest
    else:
        ho_ref, buf, sem = rest
    i = pl.program_id(0)
    nt = pl.num_programs(0)
    slot = i % 2
    tm = buf.shape[2]

    @pl.when(i == 0)
    def _():
        _start_row_gather(p0_ref, y_hbm, buf.at[0, 0], sem.at[0], tm)
        _start_row_gather(p1_ref, y_hbm, buf.at[0, 1], sem.at[0], tm)

    @pl.when(i + 1 < nt)
    def _():
        _start_row_gather(p0n_ref, y_hbm, buf.at[1 - slot, 0], sem.at[1 - slot], tm)
        _start_row_gather(p1n_ref, y_hbm, buf.at[1 - slot, 1], sem.at[1 - slot], tm)

    _wait_row_gather(y_hbm, buf.at[slot, 0], sem.at[slot], tm)
    _wait_row_gather(y_hbm, buf.at[slot, 1], sem.at[slot], tm)
    x = x_ref[...] + w0_ref[...] * buf[slot, 0] + w1_ref[...] * buf[slot, 1]
    if emit_x:
        xo_ref[...] = x
    ms = jnp.mean(x * x, axis=-1, keepdims=True)
    ho_ref[...] = (x * lax.rsqrt(ms + RMS_EPS) * g_ref[...]).astype(ho_ref.dtype)


def _combine_norm(x2d, y, pos, wts, g, emit_x, tm=256):
    t, d = x2d.shape
    nt = t // tm
    pidx = pos.reshape(EXPERT_TOPK, nt, 1, tm)
    wcol = wts.reshape(EXPERT_TOPK, t, 1)
    cur = lambda i: (i, 0, 0)
    nxt = lambda i: (jnp.minimum(i + 1, nt - 1), 0, 0)
    ispec = lambda f: pl.BlockSpec((1, 1, tm), f, memory_space=pltpu.SMEM)
    row = pl.BlockSpec((tm, d), lambda i: (i, 0))
    wspec = pl.BlockSpec((tm, 1), lambda i: (i, 0))
    if emit_x:
        out_specs = [row, row]
        out_shape = [jax.ShapeDtypeStruct((t, d), F32), jax.ShapeDtypeStruct((t, d), BF16)]
    else:
        out_specs = row
        out_shape = jax.ShapeDtypeStruct((t, d), F32)
    return pl.pallas_call(
        functools.partial(_combine_kernel, emit_x=emit_x),
        grid=(nt,),
        in_specs=[ispec(cur), ispec(cur), ispec(nxt), ispec(nxt), row, wspec, wspec,
                  pl.BlockSpec((1, d), lambda i: (0, 0)),
                  pl.BlockSpec(memory_space=pl.ANY)],
        out_specs=out_specs,
        out_shape=out_shape,
        scratch_shapes=[pltpu.VMEM((2, EXPERT_TOPK, tm, d), F32), pltpu.SemaphoreType.DMA((2,))],
        compiler_params=_params(("arbitrary",)),
        name="moe_combine_norm",
    )(pidx[0], pidx[1], pidx[0], pidx[1], x2d, wcol[0], wcol[1], g.reshape(1, d), y)


def _route(eid, n_experts):
    t = eid.shape[1]
    a = EXPERT_TOPK * t
    flat_e = eid.reshape(a)
    onehot = (flat_e[:, None] == jnp.arange(n_experts, dtype=jnp.int32)[None, :]).astype(jnp.int32)
    csum = jnp.cumsum(onehot, axis=0)
    counts = csum[-1]
    blocks = (counts + EXPERT_BLOCK - 1) // EXPERT_BLOCK
    blk_end = jnp.cumsum(blocks)
    pad_start = (blk_end - blocks) * EXPERT_BLOCK
    dest = jnp.sum(onehot * (csum - 1 + pad_start[None, :]), axis=1)
    n_blocks = -(-a // EXPERT_BLOCK) + n_experts
    p = n_blocks * EXPERT_BLOCK
    tok = jnp.arange(a, dtype=jnp.int32) % t
    slot_tok = jnp.zeros((p,), jnp.int32).at[dest].set(tok)
    block_e = jnp.sum((blk_end[None, :] <= jnp.arange(n_blocks, dtype=jnp.int32)[:, None])
                      .astype(jnp.int32), axis=1)
    block_e = jnp.minimum(block_e, n_experts - 1)
    n_used = blk_end[-1:].astype(jnp.int32)
    return slot_tok, block_e, n_used, dest.reshape(EXPERT_TOPK, t)


def kernel(x, norm1_g, w_in, b_gate, conv_w, conv_b, head_norm_g, w_out, norm2_g, w_router_group,
           b_router_group, w_router_expert, b_router_expert, w_gate, w_up, w_down, final_norm_g):
    depth = norm1_g.shape[0]
    bsz, s, d = x.shape
    t = bsz * s
    mw = d // 2
    nmh = mw // MLSTM_DV
    qkw = nmh * MLSTM_DK
    aw = d - mw
    nah = aw // ATTN_HEAD_DIM
    ngate = 4 * nmh
    n_groups = w_router_group.shape[2]
    n_experts = w_router_expert.shape[2]

    c_v = 2 * qkw
    c_o = c_v + mw
    c_g = c_o + mw
    c_a = c_g + ngate
    assert ngate < LANES and c_a + 3 * aw == w_in.shape[2]
    npad = ROUTER_ROWS - n_groups - n_experts
    w_router_t = jnp.pad(jnp.concatenate([w_router_group, w_router_expert], axis=2)
                         .transpose(0, 2, 1), ((0, 0), (0, npad), (0, 0)))
    b_router_t = jnp.pad(jnp.concatenate([b_router_group, b_router_expert], axis=1),
                         ((0, 0), (0, npad))).reshape(depth, ROUTER_ROWS, 1)
    wg_b, wu_b, wd_b = w_gate.astype(BF16), w_up.astype(BF16), w_down.astype(BF16)
    slopes = jnp.exp2(-8.0 * jnp.arange(1, nah + 1, dtype=F32) / nah)

    x2d = x.reshape(t, d)
    h = _rmsnorm(x2d, norm1_g[0], BF16)
    for l in range(depth):
        head_g = head_norm_g[l]
        qk_m = _matmul(h, w_in, l, F32, col0=0, n=2 * qkw).reshape(bsz, s, 2 * qkw)
        v_m = _matmul(h, w_in, l, BF16, col0=c_v, n=mw).reshape(bsz, s, mw)
        o_m = _matmul(h, w_in, l, F32, col0=c_o, n=mw).reshape(bsz, s, mw)
        gates = _matmul(h, w_in, l, F32, col0=c_g, n=LANES)[:, :ngate]
        q_a, k_a, v_a = (_matmul(h, w_in, l, F32, col0=c_g + i * aw, n=aw, lane_shift=ngate)
                         .reshape(bsz, s, aw) for i in range(3))

        q_m = _short_conv_silu(qk_m, conv_w[l], conv_b[l], 0, qkw, MLSTM_DK ** -0.5,
                               transpose=False)
        kt_m = _short_conv_silu(qk_m, conv_w[l], conv_b[l], qkw, qkw, 1.0, transpose=True)
        gates_t = gates.reshape(bsz, s, 4, nmh).transpose(0, 2, 3, 1).reshape(bsz, 4, nmh, 1, s)
        h_bwd = _mlstm_scan(b_gate[l], q_m, kt_m, v_m, gates_t, reverse=True)
        y_m = _mlstm_scan(b_gate[l], q_m, kt_m, v_m, gates_t, reverse=False, hprev=h_bwd,
                          ogate=o_m, head_g=head_g[:mw].reshape(1, mw))

        y_a = _dilated_attention(slopes, q_a, k_a, v_a, head_g[mw:])

        x2d = _out_proj(y_m.reshape(t, mw), y_a.reshape(t, aw), w_out, l, x2d)

        h2, eid, wts = _rmsnorm_router(x2d, norm2_g[l], w_router_t[l], b_router_t[l],
                                       n_groups, n_experts // n_groups)
        slot_tok, block_e, n_used, pos = _route(eid[:EXPERT_TOPK], n_experts)
        yb = _expert_ffn(h2, slot_tok, block_e, n_used, wg_b, wu_b, wd_b, l)
        if l + 1 < depth:
            x2d, h = _combine_norm(x2d, yb, pos, wts[:EXPERT_TOPK], norm1_g[l + 1], emit_x=True)
        else:
            out = _combine_norm(x2d, yb, pos, wts[:EXPERT_TOPK], final_norm_g, emit_x=False)
    return out.reshape(bsz, s, d)
```

```python
import functools

import jax
import jax.numpy as jnp
from jax import lax
from jax.experimental import pallas as pl
from jax.experimental.pallas import tpu as pltpu

MLSTM_DV = 512
MLSTM_DK = 256
ATTN_HEAD_DIM = 128
CONV_W = 5
DILATED_PATTERNS = ((128, 1), (512, 4), (2048, 16))
ATTN_HALF = 64
EXPERT_TOPK = 2
RMS_EPS = 1e-6
MASK_VALUE = -1e30

LANES = 128
SUBLANES = 8
VMEM_LIMIT_BYTES = 56 * 1024 * 1024
PROJ_VMEM_LIMIT_BYTES = 60 * 1024 * 1024

MLSTM_CHUNK = 256
EXPERT_BLOCK = 256
ROUTER_ROWS = 32

BF16 = jnp.bfloat16
F32 = jnp.float32


def _params(semantics):
    return pltpu.CompilerParams(dimension_semantics=semantics, vmem_limit_bytes=VMEM_LIMIT_BYTES)


def _rmsnorm_kernel(x_ref, g_ref, o_ref):
    x = x_ref[...]
    ms = jnp.mean(x * x, axis=-1, keepdims=True)
    o_ref[...] = (x * lax.rsqrt(ms + RMS_EPS) * g_ref[...]).astype(o_ref.dtype)


def _rmsnorm(x2d, g, out_dtype, tm=256):
    t, d = x2d.shape
    return pl.pallas_call(
        _rmsnorm_kernel,
        grid=(t // tm,),
        in_specs=[pl.BlockSpec((tm, d), lambda i: (i, 0)),
                  pl.BlockSpec((1, d), lambda i: (0, 0))],
        out_specs=pl.BlockSpec((tm, d), lambda i: (i, 0)),
        out_shape=jax.ShapeDtypeStruct((t, d), out_dtype),
        compiler_params=_params(("parallel",)),
        name="rmsnorm",
    )(x2d, g.reshape(1, d))


def _first_argmax(vals):
    best = vals[0]
    idx = jnp.zeros(best.shape, jnp.int32)
    for i in range(1, len(vals)):
        gt = vals[i] > best
        best = jnp.where(gt, vals[i], best)
        idx = jnp.where(gt, i, idx)
    return best, idx


def _softmax_list(vals):
    mx = functools.reduce(jnp.maximum, vals)
    es = [jnp.exp(v - mx) for v in vals]
    tot = functools.reduce(lambda a, b: a + b, es)
    return [e / tot for e in es]


def _rmsnorm_router_kernel(x_ref, g_ref, wrt_ref, brt_ref, h_ref, eid_ref, wt_ref,
                           *, n_groups, per_group):
    x = x_ref[...]
    ms = jnp.mean(x * x, axis=-1, keepdims=True)
    h = x * lax.rsqrt(ms + RMS_EPS) * g_ref[...]
    h_ref[...] = h.astype(h_ref.dtype)
    def split(v):
        hi = v.astype(BF16)
        return hi, (v - hi.astype(F32)).astype(BF16)

    def dot_nt(a, b):
        return lax.dot_general(a, b, (((1,), (1,)), ((), ())), preferred_element_type=F32)

    w_hi, w_lo = split(wrt_ref[...])
    h_hi, h_lo = split(h)
    lt = dot_nt(w_hi, h_hi) + dot_nt(w_lo, h_hi) + dot_nt(w_hi, h_lo) + brt_ref[...]
    g_prob = _softmax_list([lt[i:i + 1, :] for i in range(n_groups)])
    g_top_p, g_top = _first_argmax(g_prob)
    e_sel = []
    for j in range(per_group):
        row = lt[n_groups + j:n_groups + j + 1, :]
        for grp in range(1, n_groups):
            r = n_groups + grp * per_group + j
            row = jnp.where(g_top == grp, lt[r:r + 1, :], row)
        e_sel.append(row)
    e_prob = _softmax_list(e_sel)
    p1, i1 = _first_argmax(e_prob)
    p2, i2 = _first_argmax([jnp.where(i1 == j, -1.0, e_prob[j]) for j in range(per_group)])
    denom = p1 + p2
    w1 = g_top_p * p1 / denom
    w2 = g_top_p * p2 / denom
    e1 = g_top * per_group + i1
    e2 = g_top * per_group + i2
    rows = lax.broadcasted_iota(jnp.int32, eid_ref.shape, 0)
    eid_ref[...] = jnp.where(rows == 0, e1, jnp.where(rows == 1, e2, 0))
    wt_ref[...] = jnp.where(rows == 0, w1, jnp.where(rows == 1, w2, 0.0))


def _rmsnorm_router(x2d, g, w_router_t, b_router_t, n_groups, per_group, tm=256):
    t, d = x2d.shape
    kern = functools.partial(_rmsnorm_router_kernel, n_groups=n_groups, per_group=per_group)
    return pl.pallas_call(
        kern,
        grid=(t // tm,),
        in_specs=[pl.BlockSpec((tm, d), lambda i: (i, 0)),
                  pl.BlockSpec((1, d), lambda i: (0, 0)),
                  pl.BlockSpec((ROUTER_ROWS, d), lambda i: (0, 0)),
                  pl.BlockSpec((ROUTER_ROWS, 1), lambda i: (0, 0))],
        out_specs=[pl.BlockSpec((tm, d), lambda i: (i, 0)),
                   pl.BlockSpec((SUBLANES, tm), lambda i: (0, i)),
                   pl.BlockSpec((SUBLANES, tm), lambda i: (0, i))],
        out_shape=[jax.ShapeDtypeStruct((t, d), F32),
                   jax.ShapeDtypeStruct((SUBLANES, t), jnp.int32),
                   jax.ShapeDtypeStruct((SUBLANES, t), F32)],
        compiler_params=_params(("parallel",)),
        name="rmsnorm_router",
    )(x2d, g.reshape(1, d), w_router_t, b_router_t)


WEIGHT_CAST_ROWS = 256


def _matmul_kernel(a_ref, bt_ref, o_ref, bb_ref):
    @pl.when(pl.program_id(1) == 0)
    def _():
        step = min(WEIGHT_CAST_ROWS, bb_ref.shape[0])

        def body(c, carry):
            rows = pl.ds(pl.multiple_of(c * step, step), step)
            bb_ref[rows, :] = bt_ref[0, rows, :].astype(BF16)
            return carry

        lax.fori_loop(0, bb_ref.shape[0] // step, body, 0)

    o_ref[...] = lax.dot_general(a_ref[...], bb_ref[...], (((1,), (1,)), ((), ())),
                                 preferred_element_type=F32).astype(o_ref.dtype)


def _matmul(a, bt, layer, out_dtype, row0, n, tm=512, tn=1024):
    m, k = a.shape
    tn = min(tn, n)
    tm = min(tm, m)
    assert row0 % SUBLANES == 0 and n % tn == 0 and m % tm == 0
    return pl.pallas_call(
        _matmul_kernel,
        grid=(n // tn, m // tm),
        in_specs=[pl.BlockSpec((tm, k), lambda j, i: (i, 0)),
                  pl.BlockSpec((pl.Element(1), pl.Element(tn), pl.Element(k)),
                               lambda j, i: (layer, pl.multiple_of(row0 + j * tn, SUBLANES), 0))],
        out_specs=pl.BlockSpec((tm, tn), lambda j, i: (i, j)),
        out_shape=jax.ShapeDtypeStruct((m, n), out_dtype),
        scratch_shapes=[pltpu.VMEM((tn, k), BF16)],
        compiler_params=pltpu.CompilerParams(dimension_semantics=("parallel", "arbitrary"),
                                             vmem_limit_bytes=PROJ_VMEM_LIMIT_BYTES),
        name="proj_in",
    )(a, bt)


def _out_proj_kernel(a1_ref, a2_ref, b1_ref, b2_ref, r_ref, o_ref, bb1_ref, bb2_ref):
    @pl.when(pl.program_id(1) == 0)
    def _():
        bb1_ref[...] = b1_ref[...].astype(BF16)
        bb2_ref[...] = b2_ref[...].astype(BF16)

    acc = jnp.dot(a1_ref[...], bb1_ref[...], preferred_element_type=F32)
    acc = acc + jnp.dot(a2_ref[...], bb2_ref[...], preferred_element_type=F32)
    o_ref[...] = r_ref[...] + acc


def _out_proj(a1, a2, w, layer, res, tm=512, tn=1024):
    m, k1 = a1.shape
    k2 = a2.shape[1]
    n = w.shape[2]
    tn = min(tn, n)
    tm = min(tm, m)
    assert k1 == k2
    return pl.pallas_call(
        _out_proj_kernel,
        grid=(n // tn, m // tm),
        in_specs=[pl.BlockSpec((tm, k1), lambda j, i: (i, 0)),
                  pl.BlockSpec((tm, k2), lambda j, i: (i, 0)),
                  pl.BlockSpec((None, k1, tn), lambda j, i: (layer, 0, j)),
                  pl.BlockSpec((None, k2, tn), lambda j, i: (layer, 1, j)),
                  pl.BlockSpec((tm, tn), lambda j, i: (i, j))],
        out_specs=pl.BlockSpec((tm, tn), lambda j, i: (i, j)),
        out_shape=jax.ShapeDtypeStruct((m, n), F32),
        scratch_shapes=[pltpu.VMEM((k1, tn), BF16), pltpu.VMEM((k2, tn), BF16)],
        compiler_params=pltpu.CompilerParams(dimension_semantics=("parallel", "arbitrary"),
                                             vmem_limit_bytes=PROJ_VMEM_LIMIT_BYTES),
        name="proj_out",
    )(a1, a2, w, w, res)


def _conv_kernel(up_ref, uc_ref, un_ref, w_ref, b_ref, o_ref, buf_ref, *, ts, scale, transpose):
    t = pl.program_id(2)
    nt = pl.num_programs(2)
    pad = SUBLANES
    buf_ref[0:pad, :] = jnp.where(t > 0, up_ref[0], 0.0)
    buf_ref[pad:pad + ts, :] = uc_ref[0]
    buf_ref[pad + ts:2 * pad + ts, :] = jnp.where(t < nt - 1, un_ref[0], 0.0)
    acc = jnp.zeros((ts, buf_ref.shape[1]), F32) + b_ref[...]
    for j in range(CONV_W):
        acc = acc + buf_ref[pl.ds(pad - CONV_W // 2 + j, ts), :] * w_ref[j:j + 1, :]
    y = acc * jax.nn.sigmoid(acc) * scale
    if transpose:
        o_ref[0] = y.T.astype(o_ref.dtype)
    else:
        o_ref[0] = y.astype(o_ref.dtype)


def _short_conv_silu(u, conv_w, conv_b, col0, width, scale, transpose, ts=512):
    bsz, s, _ = u.shape
    tc = MLSTM_DK
    ts = min(ts, s)
    cb0 = col0 // tc
    nrb = ts // SUBLANES
    last_rb = s // SUBLANES - 1
    kern = functools.partial(_conv_kernel, ts=ts, scale=scale, transpose=transpose)
    if transpose:
        out_spec = pl.BlockSpec((1, tc, ts), lambda b, c, t: (b, c, t))
        out_shape = jax.ShapeDtypeStruct((bsz, width, s), BF16)
    else:
        out_spec = pl.BlockSpec((1, ts, tc), lambda b, c, t: (b, t, c))
        out_shape = jax.ShapeDtypeStruct((bsz, s, width), BF16)
    return pl.pallas_call(
        kern,
        grid=(bsz, width // tc, s // ts),
        in_specs=[pl.BlockSpec((1, SUBLANES, tc),
                               lambda b, c, t: (b, jnp.maximum(t * nrb - 1, 0), cb0 + c)),
                  pl.BlockSpec((1, ts, tc), lambda b, c, t: (b, t, cb0 + c)),
                  pl.BlockSpec((1, SUBLANES, tc),
                               lambda b, c, t: (b, jnp.minimum((t + 1) * nrb, last_rb), cb0 + c)),
                  pl.BlockSpec((CONV_W, tc), lambda b, c, t: (0, cb0 + c)),
                  pl.BlockSpec((1, tc), lambda b, c, t: (0, cb0 + c))],
        out_specs=out_spec,
        out_shape=out_shape,
        scratch_shapes=[pltpu.VMEM((ts + 2 * SUBLANES, tc), F32)],
        compiler_params=_params(("parallel", "parallel", "parallel")),
        name="conv_silu_t" if transpose else "conv_silu",
    )(u, u, u, conv_w, conv_b.reshape(1, -1))


def _log_sigmoid(x):
    return jnp.minimum(x, 0.0) - jnp.log1p(jnp.exp(-jnp.abs(x)))


def _mlstm_kernel(bias_ref, q_ref, kt_ref, v_ref, g_ref, *rest, chunk, nheads, reverse, final):
    if final:
        hprev_ref, og_ref, hg_ref, o_ref, ct_ref, m_ref, vext_ref, wi_ref, col_ref, row_ref = rest
    else:
        o_ref, ct_ref, m_ref, vext_ref, wi_ref, col_ref, row_ref = rest
    c_idx = pl.program_id(1)
    dv, dk = MLSTM_DV, MLSTM_DK
    ln = chunk

    @pl.when(c_idx == 0)
    def _():
        ct_ref[...] = jnp.zeros_like(ct_ref)
        m_ref[...] = jnp.zeros_like(m_ref)

    gi, gf = (2, 3) if reverse else (0, 1)
    row = lax.broadcasted_iota(jnp.int32, (ln, ln), 0)
    col = lax.broadcasted_iota(jnp.int32, (ln, ln), 1)
    cum_mask = ((row >= col) if reverse else (row <= col)).astype(F32)
    tri = (col >= row) if reverse else (col <= row)
    diag = row == col
    ones_cols = jnp.ones((ln, LANES), vext_ref.dtype)

    i_rows = [g_ref[gi, hh] + bias_ref[gi * nheads + hh] for hh in range(nheads)]
    f_rows = [_log_sigmoid(g_ref[gf, hh] + bias_ref[gf * nheads + hh]) for hh in range(nheads)]
    f_all = jnp.concatenate(f_rows + [jnp.zeros((SUBLANES - nheads, ln), F32)], axis=0)
    b_all = jnp.dot(f_all, cum_mask, precision=lax.Precision.HIGHEST, preferred_element_type=F32)
    for hh in range(nheads):
        i_row = i_rows[hh]
        b_row = b_all[hh:hh + 1, :]
        total = b_row[:, 0:1] if reverse else b_row[:, ln - 1:ln]
        b_col = jnp.sum(jnp.where(diag, b_row, 0.0), axis=1, keepdims=True)
        m0 = m_ref[hh, 0:1, 0:1]
        d_log = jnp.where(tri, b_col - b_row + i_row, MASK_VALUE)
        a_col = b_col + m0
        m_t = jnp.maximum(a_col, jnp.max(d_log, axis=1, keepdims=True))
        wi_ref[hh] = jnp.exp(d_log - m_t)
        col_ref[hh, :, 0:1] = jnp.exp(a_col - m_t)
        col_ref[hh, :, 1:2] = jnp.exp(-m_t)
        g_row = total - b_row + i_row
        m_new = jnp.maximum(total + m0, jnp.max(g_row, axis=1, keepdims=True))
        row_ref[hh, 0:1, :] = jnp.exp(g_row - m_new)
        row_ref[hh, 1:2, :] = jnp.broadcast_to(jnp.exp(total + m0 - m_new), (1, ln))
        m_ref[hh] = jnp.broadcast_to(m_new, m_ref.shape[1:])

    for hh in range(nheads):
        w_inter = col_ref[hh, :, 0:1]
        inv_floor = col_ref[hh, :, 1:2]
        w_add = row_ref[hh, 0:1, :]
        decay = row_ref[hh, 1:2, 0:1]
        q = q_ref[0, :, hh * dk:(hh + 1) * dk]
        kt = kt_ref[0, hh * dk:(hh + 1) * dk, :]
        vext_ref[hh, :, 0:dv] = v_ref[0, :, hh * dv:(hh + 1) * dv]
        vext_ref[hh, :, dv:] = ones_cols
        vext = vext_ref[hh]

        s_qk = jnp.dot(q, kt, preferred_element_type=F32) * wi_ref[hh]
        intra = jnp.dot(s_qk.astype(BF16), vext, preferred_element_type=F32)
        ct = ct_ref[hh]
        inter = jnp.dot(q, ct.astype(BF16), preferred_element_type=F32)
        tot = intra + w_inter * inter
        den = tot[:, dv:dv + 1]
        h = tot[:, 0:dv] / jnp.maximum(jnp.abs(den), inv_floor)

        ktw = (kt.astype(F32) * w_add).astype(BF16)
        ct_ref[hh] = decay * ct + jnp.dot(ktw, vext, preferred_element_type=F32)

        cols = slice(hh * dv, (hh + 1) * dv)
        if final:
            hs = h + hprev_ref[0, :, cols]
            hn = hs * lax.rsqrt(jnp.mean(hs * hs, axis=-1, keepdims=True) + RMS_EPS)
            o_ref[0, :, cols] = (hn * hg_ref[:, cols]
                                 * jax.nn.sigmoid(og_ref[0, :, cols])).astype(o_ref.dtype)
        else:
            o_ref[0, :, cols] = h


def _mlstm_scan(bias, q, kt, v, gates_t, reverse, hprev=None, ogate=None, head_g=None):
    bsz, s, _ = q.shape
    nheads = v.shape[2] // MLSTM_DV
    ln = min(MLSTM_CHUNK, s)
    nc = s // ln
    final = hprev is not None

    def cidx(c):
        return (nc - 1 - c) if reverse else c

    qkw, vw = nheads * MLSTM_DK, nheads * MLSTM_DV
    wide = pl.BlockSpec((1, ln, vw), lambda b, c: (b, cidx(c), 0))
    in_specs = [pl.BlockSpec(memory_space=pltpu.SMEM),
                pl.BlockSpec((1, ln, qkw), lambda b, c: (b, cidx(c), 0)),
                pl.BlockSpec((1, qkw, ln), lambda b, c: (b, 0, cidx(c))),
                wide,
                pl.BlockSpec((None, 4, nheads, 1, ln), lambda b, c: (b, 0, 0, 0, cidx(c)))]
    args = [bias, q, kt, v, gates_t]
    if final:
        in_specs += [wide, wide, pl.BlockSpec((1, vw), lambda b, c: (0, 0))]
        args += [hprev, ogate, head_g]
    kern = functools.partial(_mlstm_kernel, chunk=ln, nheads=nheads, reverse=reverse, final=final)
    return pl.pallas_call(
        kern,
        grid=(bsz, nc),
        in_specs=in_specs,
        out_specs=wide,
        out_shape=jax.ShapeDtypeStruct((bsz, s, vw), BF16 if final else F32),
        scratch_shapes=[pltpu.VMEM((nheads, MLSTM_DK, MLSTM_DV + LANES), F32),
                        pltpu.VMEM((nheads, SUBLANES, LANES), F32),
                        pltpu.VMEM((nheads, ln, MLSTM_DV + LANES), BF16),
                        pltpu.VMEM((nheads, ln, ln), F32),
                        pltpu.VMEM((nheads, ln, LANES), F32),
                        pltpu.VMEM((nheads, SUBLANES, ln), F32)],
        compiler_params=_params(("parallel", "arbitrary")),
        name="mlstm_fwd" if final else "mlstm_bwd",
    )(*args)


ATTN_TILE = 1024
ATTN_HALO = ATTN_HALF * 16
ATTN_SUB = 128


def _band_bias(nq, nk, slope_d):
    qi = lax.broadcasted_iota(jnp.int32, (nq, nk), 0)
    kj = lax.broadcasted_iota(jnp.int32, (nq, nk), 1)
    rel = jnp.abs(kj - ATTN_HALF - qi)
    return jnp.where(rel <= ATTN_HALF, -slope_d * rel.astype(F32), MASK_VALUE)


def _band_scores(qs, ks, bias, key_lo, key_hi):
    s = lax.dot_general(qs.astype(BF16), ks.astype(BF16), (((1,), (1,)), ((), ())),
                        preferred_element_type=F32) + bias
    if key_lo is not None or key_hi is not None:
        kj = lax.broadcasted_iota(jnp.int32, s.shape, 1)
        inside = None
        if key_lo is not None:
            inside = kj >= key_lo
        if key_hi is not None:
            inside = (kj < key_hi) if inside is None else (inside & (kj < key_hi))
        s = jnp.where(inside, s, MASK_VALUE)
    m = jnp.max(s, axis=1, keepdims=True)
    return s, jnp.broadcast_to(m, (s.shape[0], LANES))


def _band_softmax_pv(s, m_b, vs):
    nq, nk = s.shape
    reps = [m_b] * (nk // LANES) + ([m_b[:, :nk % LANES]] if nk % LANES else [])
    m_wide = reps[0] if len(reps) == 1 else jnp.concatenate(reps, axis=1)
    p = jnp.exp(s - m_wide).astype(BF16)
    v_ext = jnp.concatenate([vs.astype(BF16), jnp.ones((nk, LANES), BF16)], axis=1)
    r = jnp.dot(p, v_ext, preferred_element_type=F32)
    l_b = r[:, ATTN_HEAD_DIM:]
    return r[:, :ATTN_HEAD_DIM] / l_b, m_b + jnp.log(l_b)


def _attn_kernel(slopes_ref, q_ref, kp_ref, kc_ref, kn_ref, vp_ref, vc_ref, vn_ref, g_ref, y_ref,
                 q4, k4, v4, acc1, lse1, acc2, lse2, acc3, lse3, ynat, sbuf, mbuf, *, tq):
    t = pl.program_id(1)
    half = ATTN_HALF
    dh = ATTN_HEAD_DIM
    halo = ATTN_HALO
    n4, h4 = tq // 4, halo // 4
    n16 = tq // 16
    sub = ATTN_SUB
    scale = ATTN_HEAD_DIM ** -0.5
    first_tile = t == 0
    last_tile = t == pl.num_programs(1) - 1
    slope = slopes_ref[pl.program_id(2)]

    for c in range(4):
        q4[c] = q_ref[0, pl.ds(c, n4, stride=4), :] * scale
        for dst, (p_ref, c_ref, n_ref) in ((k4, (kp_ref, kc_ref, kn_ref)),
                                           (v4, (vp_ref, vc_ref, vn_ref))):
            dst[c, 0:h4, :] = p_ref[0, pl.ds(c, h4, stride=4), :]
            dst[c, h4:h4 + n4, :] = c_ref[0, pl.ds(c, n4, stride=4), :]
            dst[c, h4 + n4:, :] = n_ref[0, pl.ds(c, h4, stride=4), :]

    def bounds(j, nblk, nsub):
        lo = jnp.where(first_tile, half, 0) if j == 0 else None
        hi = jnp.where(last_tile, nsub + half, nsub + 2 * half) if j == nblk - 1 else None
        return lo, hi

    def run_blocks(blocks, nsub, slope_d):
        nk = nsub + 2 * half
        bias = _band_bias(nsub, nk, slope_d)
        for i, (load_q, load_k, _, lo, hi, _) in enumerate(blocks):
            s, m_b = _band_scores(load_q(), load_k(), bias, lo, hi)
            sbuf[i, 0:nsub, 0:nk] = s
            mbuf[i, 0:nsub, :] = m_b
        for i, (_, _, load_v, _, _, store) in enumerate(blocks):
            store(*_band_softmax_pv(sbuf[i, 0:nsub, 0:nk], mbuf[i, 0:nsub, :], load_v()))

    def window(refs, r0, nk):
        p_ref, c_ref, n_ref = refs
        parts = []
        if r0 < 0:
            parts.append(p_ref[0, halo + r0:halo, :])
        parts.append(c_ref[0, max(r0, 0):min(r0 + nk, tq), :])
        if r0 + nk > tq:
            parts.append(n_ref[0, 0:r0 + nk - tq, :])
        return parts[0] if len(parts) == 1 else jnp.concatenate(parts, axis=0)

    def store_to(acc, lse, *idx):
        def store(o, lse_b):
            acc[idx] = o
            lse[idx] = lse_b
        return store

    krefs, vrefs = (kp_ref, kc_ref, kn_ref), (vp_ref, vc_ref, vn_ref)
    nk = sub + 2 * half
    nblk = tq // sub
    blocks = []
    for j in range(nblk):
        rows = slice(j * sub, (j + 1) * sub)
        blocks.append((lambda rows=rows: q_ref[0, rows, :] * scale,
                       lambda j=j: window(krefs, j * sub - half, nk),
                       lambda j=j: window(vrefs, j * sub - half, nk),
                       *bounds(j, nblk, sub), store_to(acc1, lse1, rows, slice(None))))
    run_blocks(blocks, sub, slope)

    nblk = n4 // sub
    blocks = []
    for c in range(4):
        for j in range(nblk):
            rows = slice(j * sub, (j + 1) * sub)
            krows = slice(h4 + j * sub - half, h4 + j * sub - half + nk)
            blocks.append((lambda c=c, rows=rows: q4[c, rows, :],
                           lambda c=c, krows=krows: k4[c, krows, :],
                           lambda c=c, krows=krows: v4[c, krows, :],
                           *bounds(j, nblk, sub), store_to(acc2, lse2, c, rows, slice(None))))
    run_blocks(blocks, sub, slope * 4)

    nsub = min(sub, n16)
    nk = nsub + 2 * half
    nblk = n16 // nsub
    blocks = []
    for c in range(4):
        for c16 in range(4):
            for j in range(nblk):
                rows_q = pl.ds(c16 + 4 * j * nsub, nsub, stride=4)
                rows_k = pl.ds(h4 + c16 + 4 * (j * nsub - half), nk, stride=4)
                blocks.append((lambda c=c, rows_q=rows_q: q4[c, rows_q, :],
                               lambda c=c, rows_k=rows_k: k4[c, rows_k, :],
                               lambda c=c, rows_k=rows_k: v4[c, rows_k, :],
                               *bounds(j, nblk, nsub),
                               store_to(acc3, lse3, c, rows_q, slice(None))))
    run_blocks(blocks, nsub, slope * 16)

    for c in range(4):
        rows = pl.ds(c, n4, stride=4)
        l1, l2, l3 = lse1[rows, :], lse2[c], lse3[c]
        mx = jnp.maximum(jnp.maximum(l1, l2), l3)
        e1, e2, e3 = jnp.exp(l1 - mx), jnp.exp(l2 - mx), jnp.exp(l3 - mx)
        o = (e1 * acc1[rows, :] + e2 * acc2[c] + e3 * acc3[c]) / (e1 + e2 + e3)
        on = o * lax.rsqrt(jnp.mean(o * o, axis=-1, keepdims=True) + RMS_EPS)
        ynat[rows, :] = on * g_ref[...]
    y_ref[0] = ynat[...].astype(y_ref.dtype)


def _dilated_attention(slopes, q, k, v, head_g):
    bsz, s, width = q.shape
    tq = min(ATTN_TILE, s)
    halo = ATTN_HALO
    assert tq % halo == 0 and s % tq == 0 and (tq // 16) % min(ATTN_SUB, tq // 16) == 0
    dh = ATTN_HEAD_DIM
    nhb = tq // halo
    last_hb = s // halo - 1
    n4, h4 = tq // 4, halo // 4
    cur = pl.BlockSpec((1, tq, dh), lambda b, t, c, sl: (b, t, c))
    prev = pl.BlockSpec((1, halo, dh), lambda b, t, c, sl: (b, jnp.maximum(t * nhb - 1, 0), c))
    nxt = pl.BlockSpec((1, halo, dh),
                       lambda b, t, c, sl: (b, jnp.minimum((t + 1) * nhb, last_hb), c))
    kern = functools.partial(_attn_kernel, tq=tq)
    nstage = max(tq // ATTN_SUB, 16 * ((tq // 16) // min(ATTN_SUB, tq // 16)))
    win = pltpu.VMEM((4, n4 + 2 * h4, dh), F32)
    cls = pltpu.VMEM((4, n4, dh), F32)
    nat = pltpu.VMEM((tq, dh), F32)
    return pl.pallas_call(
        kern,
        grid_spec=pltpu.PrefetchScalarGridSpec(
            num_scalar_prefetch=1,
            grid=(bsz, s // tq, width // dh),
            in_specs=[cur, prev, cur, nxt, prev, cur, nxt,
                      pl.BlockSpec((1, dh), lambda b, t, c, sl: (0, c))],
            out_specs=pl.BlockSpec((1, tq, dh), lambda b, t, c, sl: (b, t, c)),
            scratch_shapes=[cls, win, win, nat, nat, cls, cls, cls, cls, nat,
                            pltpu.VMEM((nstage, ATTN_SUB, ATTN_SUB + 2 * ATTN_HALF), F32),
                            pltpu.VMEM((nstage, ATTN_SUB, LANES), F32)]),
        out_shape=jax.ShapeDtypeStruct((bsz, s, width), BF16),
        compiler_params=_params(("parallel", "parallel", "parallel")),
        name="dilated_attn",
    )(slopes, q, k, k, k, v, v, v, head_g.reshape(1, width))


def _start_row_gather(idx_ref, src_hbm, dst, sem, nrows):
    def body(r2, carry):
        for prio in range(2):
            r = 2 * r2 + prio
            pltpu.make_async_copy(src_hbm.at[pl.ds(idx_ref[0, 0, r], 1), :],
                                  dst.at[pl.ds(r, 1), :], sem).start(priority=prio)
        return carry

    lax.fori_loop(0, nrows // 2, body, 0, unroll=4)


def _wait_row_gather(src_hbm, dst, sem, nrows):
    pltpu.make_async_copy(src_hbm.at[pl.ds(0, nrows), :], dst, sem).wait()


def _expert_up_kernel(be_ref, nb_ref, idx_ref, idx_next_ref, h_hbm, wg_ref, wu_ref, o_ref,
                      xbuf, sem):
    i = pl.program_id(0)
    n_used = nb_ref[0]
    slot = i % 2
    blk = xbuf.shape[1]

    @pl.when(i == 0)
    def _():
        _start_row_gather(idx_ref, h_hbm, xbuf.at[0], sem.at[0], blk)

    @pl.when(i + 1 < n_used)
    def _():
        _start_row_gather(idx_next_ref, h_hbm, xbuf.at[1 - slot], sem.at[1 - slot], blk)

    @pl.when(i < n_used)
    def _():
        _wait_row_gather(h_hbm, xbuf.at[slot], sem.at[slot], blk)
        x = xbuf[slot].astype(BF16)
        g = jnp.dot(x, wg_ref[...], preferred_element_type=F32)
        u = jnp.dot(x, wu_ref[...], preferred_element_type=F32)
        o_ref[...] = (g * jax.nn.sigmoid(g) * u).astype(o_ref.dtype)

    @pl.when(i >= n_used)
    def _():
        o_ref[...] = jnp.zeros_like(o_ref)


def _expert_down_kernel(be_ref, nb_ref, h_ref, wd_ref, o_ref):
    i = pl.program_id(0)

    @pl.when(i < nb_ref[0])
    def _():
        o_ref[...] = jnp.dot(h_ref[...], wd_ref[...], preferred_element_type=F32)

    @pl.when(i >= nb_ref[0])
    def _():
        o_ref[...] = jnp.zeros_like(o_ref)


def _expert_ffn(h, slot_tok, block_e, n_used, w_gate, w_up, w_down, layer):
    d = h.shape[1]
    p = slot_tok.shape[0]
    ff = w_gate.shape[3]
    blk = EXPERT_BLOCK
    nblk = p // blk
    idx = slot_tok.reshape(nblk, 1, blk)
    h1 = pl.pallas_call(
        _expert_up_kernel,
        grid_spec=pltpu.PrefetchScalarGridSpec(
            num_scalar_prefetch=2,
            grid=(nblk,),
            in_specs=[pl.BlockSpec((1, 1, blk), lambda i, be, nb: (i, 0, 0),
                                   memory_space=pltpu.SMEM),
                      pl.BlockSpec((1, 1, blk), lambda i, be, nb: (jnp.minimum(i + 1, nblk - 1), 0, 0),
                                   memory_space=pltpu.SMEM),
                      pl.BlockSpec(memory_space=pl.ANY),
                      pl.BlockSpec((None, None, d, ff), lambda i, be, nb: (layer, be[i], 0, 0)),
                      pl.BlockSpec((None, None, d, ff), lambda i, be, nb: (layer, be[i], 0, 0))],
            out_specs=pl.BlockSpec((blk, ff), lambda i, be, nb: (i, 0)),
            scratch_shapes=[pltpu.VMEM((2, blk, d), F32), pltpu.SemaphoreType.DMA((2,))]),
        out_shape=jax.ShapeDtypeStruct((p, ff), BF16),
        compiler_params=_params(("arbitrary",)),
        name="expert_up",
    )(block_e, n_used, idx, idx, h, w_gate, w_up)
    return pl.pallas_call(
        _expert_down_kernel,
        grid_spec=pltpu.PrefetchScalarGridSpec(
            num_scalar_prefetch=2,
            grid=(nblk,),
            in_specs=[pl.BlockSpec((blk, ff), lambda i, be, nb: (i, 0)),
                      pl.BlockSpec((None, None, ff, d), lambda i, be, nb: (layer, be[i], 0, 0))],
            out_specs=pl.BlockSpec((blk, d), lambda i, be, nb: (i, 0))),
        out_shape=jax.ShapeDtypeStruct((p, d), F32),
        compiler_params=_params(("arbitrary",)),
        name="expert_down",
    )(block_e, n_used, h1, w_down)


def _combine_kernel(p0_ref, p1_ref, p0n_ref, p1n_ref, x_ref, w0_ref, w1_ref, g_ref, y_hbm,
                    *rest, emit_x):
    if emit_x:
        xo_ref, ho_ref, buf, sem = rest
    else:
        ho_ref, buf, sem = rest
    i = pl.program_id(0)
    nt = pl.num_programs(0)
    slot = i % 2
    tm = buf.shape[2]

    @pl.when(i == 0)
    def _():
        _start_row_gather(p0_ref, y_hbm, buf.at[0, 0], sem.at[0], tm)
        _start_row_gather(p1_ref, y_hbm, buf.at[0, 1], sem.at[0], tm)

    @pl.when(i + 1 < nt)
    def _():
        _start_row_gather(p0n_ref, y_hbm, buf.at[1 - slot, 0], sem.at[1 - slot], tm)
        _start_row_gather(p1n_ref, y_hbm, buf.at[1 - slot, 1], sem.at[1 - slot], tm)

    _wait_row_gather(y_hbm, buf.at[slot, 0], sem.at[slot], tm)
    _wait_row_gather(y_hbm, buf.at[slot, 1], sem.at[slot], tm)
    x = x_ref[...] + w0_ref[...] * buf[slot, 0] + w1_ref[...] * buf[slot, 1]
    if emit_x:
        xo_ref[...] = x
    ms = jnp.mean(x * x, axis=-1, keepdims=True)
    ho_ref[...] = (x * lax.rsqrt(ms + RMS_EPS) * g_ref[...]).astype(ho_ref.dtype)


def _combine_norm(x2d, y, pos, wts, g, emit_x, tm=256):
    t, d = x2d.shape
    nt = t // tm
    pidx = pos.reshape(EXPERT_TOPK, nt, 1, tm)
    wcol = wts.reshape(EXPERT_TOPK, t, 1)
    cur = lambda i: (i, 0, 0)
    nxt = lambda i: (jnp.minimum(i + 1, nt - 1), 0, 0)
    ispec = lambda f: pl.BlockSpec((1, 1, tm), f, memory_space=pltpu.SMEM)
    row = pl.BlockSpec((tm, d), lambda i: (i, 0))
    wspec = pl.BlockSpec((tm, 1), lambda i: (i, 0))
    if emit_x:
        out_specs = [row, row]
        out_shape = [jax.ShapeDtypeStruct((t, d), F32), jax.ShapeDtypeStruct((t, d), BF16)]
    else:
        out_specs = row
        out_shape = jax.ShapeDtypeStruct((t, d), F32)
    return pl.pallas_call(
        functools.partial(_combine_kernel, emit_x=emit_x),
        grid=(nt,),
        in_specs=[ispec(cur), ispec(cur), ispec(nxt), ispec(nxt), row, wspec, wspec,
                  pl.BlockSpec((1, d), lambda i: (0, 0)),
                  pl.BlockSpec(memory_space=pl.ANY)],
        out_specs=out_specs,
        out_shape=out_shape,
        scratch_shapes=[pltpu.VMEM((2, EXPERT_TOPK, tm, d), F32), pltpu.SemaphoreType.DMA((2,))],
        compiler_params=_params(("arbitrary",)),
        name="moe_combine_norm",
    )(pidx[0], pidx[1], pidx[0], pidx[1], x2d, wcol[0], wcol[1], g.reshape(1, d), y)


def _route(eid, n_experts):
    t = eid.shape[1]
    a = EXPERT_TOPK * t
    flat_e = eid.reshape(a)
    onehot = (flat_e[:, None] == jnp.arange(n_experts, dtype=jnp.int32)[None, :]).astype(jnp.int32)
    csum = jnp.cumsum(onehot, axis=0)
    counts = csum[-1]
    blocks = (counts + EXPERT_BLOCK - 1) // EXPERT_BLOCK
    blk_end = jnp.cumsum(blocks)
    pad_start = (blk_end - blocks) * EXPERT_BLOCK
    dest = jnp.sum(onehot * (csum - 1 + pad_start[None, :]), axis=1)
    n_blocks = -(-a // EXPERT_BLOCK) + n_experts
    p = n_blocks * EXPERT_BLOCK
    tok = jnp.arange(a, dtype=jnp.int32) % t
    slot_tok = jnp.zeros((p,), jnp.int32).at[dest].set(tok)
    block_e = jnp.sum((blk_end[None, :] <= jnp.arange(n_blocks, dtype=jnp.int32)[:, None])
                      .astype(jnp.int32), axis=1)
    block_e = jnp.minimum(block_e, n_experts - 1)
    n_used = blk_end[-1:].astype(jnp.int32)
    return slot_tok, block_e, n_used, dest.reshape(EXPERT_TOPK, t)


def kernel(x, norm1_g, w_in, b_gate, conv_w, conv_b, head_norm_g, w_out, norm2_g, w_router_group,
           b_router_group, w_router_expert, b_router_expert, w_gate, w_up, w_down, final_norm_g):
    depth = norm1_g.shape[0]
    bsz, s, d = x.shape
    t = bsz * s
    mw = d // 2
    nmh = mw // MLSTM_DV
    qkw = nmh * MLSTM_DK
    aw = d - mw
    nah = aw // ATTN_HEAD_DIM
    ngate = 4 * nmh
    n_groups = w_router_group.shape[2]
    n_experts = w_router_expert.shape[2]

    c_v = 2 * qkw
    c_o = c_v + mw
    c_g = c_o + mw
    c_a = c_g + ngate
    assert ngate <= LANES and c_a + 3 * aw == w_in.shape[2]
    w_in_t = jnp.swapaxes(w_in, 1, 2)
    npad = ROUTER_ROWS - n_groups - n_experts
    w_router_t = jnp.pad(jnp.concatenate([w_router_group, w_router_expert], axis=2)
                         .transpose(0, 2, 1), ((0, 0), (0, npad), (0, 0)))
    b_router_t = jnp.pad(jnp.concatenate([b_router_group, b_router_expert], axis=1),
                         ((0, 0), (0, npad))).reshape(depth, ROUTER_ROWS, 1)
    wg_b, wu_b, wd_b = w_gate.astype(BF16), w_up.astype(BF16), w_down.astype(BF16)
    slopes = jnp.exp2(-8.0 * jnp.arange(1, nah + 1, dtype=F32) / nah)

    x2d = x.reshape(t, d)
    h = _rmsnorm(x2d, norm1_g[0], BF16)
    for l in range(depth):
        head_g = head_norm_g[l]
        qk_m = _matmul(h, w_in_t, l, F32, row0=0, n=2 * qkw).reshape(bsz, s, 2 * qkw)
        v_m = _matmul(h, w_in_t, l, BF16, row0=c_v, n=mw).reshape(bsz, s, mw)
        o_m = _matmul(h, w_in_t, l, F32, row0=c_o, n=mw).reshape(bsz, s, mw)
        gates = _matmul(h, w_in_t, l, F32, row0=c_g, n=LANES)[:, :ngate]
        q_a, k_a, v_a = (_matmul(h, w_in_t, l, F32, row0=c_a + i * aw, n=aw)
                         .reshape(bsz, s, aw) for i in range(3))

        q_m = _short_conv_silu(qk_m, conv_w[l], conv_b[l], 0, qkw, MLSTM_DK ** -0.5,
                               transpose=False)
        kt_m = _short_conv_silu(qk_m, conv_w[l], conv_b[l], qkw, qkw, 1.0, transpose=True)
        gates_t = gates.reshape(bsz, s, 4, nmh).transpose(0, 2, 3, 1).reshape(bsz, 4, nmh, 1, s)
        h_bwd = _mlstm_scan(b_gate[l], q_m, kt_m, v_m, gates_t, reverse=True)
        y_m = _mlstm_scan(b_gate[l], q_m, kt_m, v_m, gates_t, reverse=False, hprev=h_bwd,
                          ogate=o_m, head_g=head_g[:mw].reshape(1, mw))

        y_a = _dilated_attention(slopes, q_a, k_a, v_a, head_g[mw:])

        x2d = _out_proj(y_m.reshape(t, mw), y_a.reshape(t, aw), w_out, l, x2d)

        h2, eid, wts = _rmsnorm_router(x2d, norm2_g[l], w_router_t[l], b_router_t[l],
                                       n_groups, n_experts // n_groups)
        slot_tok, block_e, n_used, pos = _route(eid[:EXPERT_TOPK], n_experts)
        yb = _expert_ffn(h2, slot_tok, block_e, n_used, wg_b, wu_b, wd_b, l)
        if l + 1 < depth:
            x2d, h = _combine_norm(x2d, yb, pos, wts[:EXPERT_TOPK], norm1_g[l + 1], emit_x=True)
        else:
            out = _combine_norm(x2d, yb, pos, wts[:EXPERT_TOPK], final_norm_g, emit_x=False)
    return out.reshape(bsz, s, d)
```

```python
import functools

import jax
import jax.numpy as jnp
from jax import lax
from jax.experimental import pallas as pl
from jax.experimental.pallas import tpu as pltpu

MLSTM_DV = 512
MLSTM_DK = 256
ATTN_HEAD_DIM = 128
CONV_W = 5
DILATED_PATTERNS = ((128, 1), (512, 4), (2048, 16))
ATTN_HALF = 64
EXPERT_TOPK = 2
RMS_EPS = 1e-6
MASK_VALUE = -1e30
LOG2E = 1.4426950408889634

LANES = 128
SUBLANES = 8
VMEM_LIMIT_BYTES = 56 * 1024 * 1024
PROJ_VMEM_LIMIT_BYTES = 60 * 1024 * 1024

MLSTM_CHUNK = 256
EXPERT_BLOCK = 256
ROUTER_ROWS = 32

BF16 = jnp.bfloat16
F32 = jnp.float32


def _params(semantics):
    return pltpu.CompilerParams(dimension_semantics=semantics, vmem_limit_bytes=VMEM_LIMIT_BYTES)


def _rmsnorm_kernel(x_ref, g_ref, o_ref):
    x = x_ref[...]
    ms = jnp.mean(x * x, axis=-1, keepdims=True)
    o_ref[...] = (x * lax.rsqrt(ms + RMS_EPS) * g_ref[...]).astype(o_ref.dtype)


def _rmsnorm(x2d, g, out_dtype, tm=256):
    t, d = x2d.shape
    return pl.pallas_call(
        _rmsnorm_kernel,
        grid=(t // tm,),
        in_specs=[pl.BlockSpec((tm, d), lambda i: (i, 0)),
                  pl.BlockSpec((1, d), lambda i: (0, 0))],
        out_specs=pl.BlockSpec((tm, d), lambda i: (i, 0)),
        out_shape=jax.ShapeDtypeStruct((t, d), out_dtype),
        compiler_params=_params(("parallel",)),
        name="rmsnorm",
    )(x2d, g.reshape(1, d))


def _first_argmax(vals):
    best = vals[0]
    idx = jnp.zeros(best.shape, jnp.int32)
    for i in range(1, len(vals)):
        gt = vals[i] > best
        best = jnp.where(gt, vals[i], best)
        idx = jnp.where(gt, i, idx)
    return best, idx


def _softmax_list(vals):
    mx = functools.reduce(jnp.maximum, vals)
    es = [jnp.exp(v - mx) for v in vals]
    tot = functools.reduce(lambda a, b: a + b, es)
    return [e / tot for e in es]


def _rmsnorm_router_kernel(x_ref, g_ref, wrt_ref, brt_ref, h_ref, eid_ref, wt_ref,
                           *, n_groups, per_group):
    x = x_ref[...]
    ms = jnp.mean(x * x, axis=-1, keepdims=True)
    h = x * lax.rsqrt(ms + RMS_EPS) * g_ref[...]
    h_ref[...] = h.astype(h_ref.dtype)
    def split(v):
        hi = v.astype(BF16)
        return hi, (v - hi.astype(F32)).astype(BF16)

    def dot_nt(a, b):
        return lax.dot_general(a, b, (((1,), (1,)), ((), ())), preferred_element_type=F32)

    w_hi, w_lo = split(wrt_ref[...])
    h_hi, h_lo = split(h)
    lt = dot_nt(w_hi, h_hi) + dot_nt(w_lo, h_hi) + dot_nt(w_hi, h_lo) + brt_ref[...]
    g_prob = _softmax_list([lt[i:i + 1, :] for i in range(n_groups)])
    g_top_p, g_top = _first_argmax(g_prob)
    e_sel = []
    for j in range(per_group):
        row = lt[n_groups + j:n_groups + j + 1, :]
        for grp in range(1, n_groups):
            r = n_groups + grp * per_group + j
            row = jnp.where(g_top == grp, lt[r:r + 1, :], row)
        e_sel.append(row)
    e_prob = _softmax_list(e_sel)
    p1, i1 = _first_argmax(e_prob)
    p2, i2 = _first_argmax([jnp.where(i1 == j, -1.0, e_prob[j]) for j in range(per_group)])
    denom = p1 + p2
    w1 = g_top_p * p1 / denom
    w2 = g_top_p * p2 / denom
    e1 = g_top * per_group + i1
    e2 = g_top * per_group + i2
    rows = lax.broadcasted_iota(jnp.int32, eid_ref.shape, 0)
    eid_ref[...] = jnp.where(rows == 0, e1, jnp.where(rows == 1, e2, 0))
    wt_ref[...] = jnp.where(rows == 0, w1, jnp.where(rows == 1, w2, 0.0))


def _rmsnorm_router(x2d, g, w_router_t, b_router_t, n_groups, per_group, tm=256):
    t, d = x2d.shape
    kern = functools.partial(_rmsnorm_router_kernel, n_groups=n_groups, per_group=per_group)
    return pl.pallas_call(
        kern,
        grid=(t // tm,),
        in_specs=[pl.BlockSpec((tm, d), lambda i: (i, 0)),
                  pl.BlockSpec((1, d), lambda i: (0, 0)),
                  pl.BlockSpec((ROUTER_ROWS, d), lambda i: (0, 0)),
                  pl.BlockSpec((ROUTER_ROWS, 1), lambda i: (0, 0))],
        out_specs=[pl.BlockSpec((tm, d), lambda i: (i, 0)),
                   pl.BlockSpec((SUBLANES, tm), lambda i: (0, i)),
                   pl.BlockSpec((SUBLANES, tm), lambda i: (0, i))],
        out_shape=[jax.ShapeDtypeStruct((t, d), F32),
                   jax.ShapeDtypeStruct((SUBLANES, t), jnp.int32),
                   jax.ShapeDtypeStruct((SUBLANES, t), F32)],
        compiler_params=_params(("parallel",)),
        name="rmsnorm_router",
    )(x2d, g.reshape(1, d), w_router_t, b_router_t)


WEIGHT_CAST_ROWS = 256


def _matmul_kernel(a_ref, bt_ref, o_ref, bb_ref):
    @pl.when(pl.program_id(1) == 0)
    def _():
        step = min(WEIGHT_CAST_ROWS, bb_ref.shape[0])

        def body(c, carry):
            rows = pl.ds(pl.multiple_of(c * step, step), step)
            bb_ref[rows, :] = bt_ref[0, rows, :].astype(BF16)
            return carry

        lax.fori_loop(0, bb_ref.shape[0] // step, body, 0)

    o_ref[...] = lax.dot_general(a_ref[...], bb_ref[...], (((1,), (1,)), ((), ())),
                                 preferred_element_type=F32).astype(o_ref.dtype)


def _matmul(a, bt, layer, out_dtype, row0, n, tm=512, tn=1024):
    m, k = a.shape
    tn = min(tn, n)
    tm = min(tm, m)
    assert row0 % SUBLANES == 0 and n % tn == 0 and m % tm == 0
    return pl.pallas_call(
        _matmul_kernel,
        grid=(n // tn, m // tm),
        in_specs=[pl.BlockSpec((tm, k), lambda j, i: (i, 0)),
                  pl.BlockSpec((pl.Element(1), pl.Element(tn), pl.Element(k)),
                               lambda j, i: (layer, pl.multiple_of(row0 + j * tn, SUBLANES), 0))],
        out_specs=pl.BlockSpec((tm, tn), lambda j, i: (i, j)),
        out_shape=jax.ShapeDtypeStruct((m, n), out_dtype),
        scratch_shapes=[pltpu.VMEM((tn, k), BF16)],
        compiler_params=pltpu.CompilerParams(dimension_semantics=("parallel", "arbitrary"),
                                             vmem_limit_bytes=PROJ_VMEM_LIMIT_BYTES),
        name="proj_in",
    )(a, bt)


def _out_proj_kernel(a1_ref, a2_ref, b1_ref, b2_ref, r_ref, o_ref, bb1_ref, bb2_ref):
    @pl.when(pl.program_id(1) == 0)
    def _():
        bb1_ref[...] = b1_ref[...].astype(BF16)
        bb2_ref[...] = b2_ref[...].astype(BF16)

    acc = jnp.dot(a1_ref[...], bb1_ref[...], preferred_element_type=F32)
    acc = acc + jnp.dot(a2_ref[...], bb2_ref[...], preferred_element_type=F32)
    o_ref[...] = r_ref[...] + acc


def _out_proj(a1, a2, w, layer, res, tm=512, tn=1024):
    m, k1 = a1.shape
    k2 = a2.shape[1]
    n = w.shape[2]
    tn = min(tn, n)
    tm = min(tm, m)
    assert k1 == k2
    return pl.pallas_call(
        _out_proj_kernel,
        grid=(n // tn, m // tm),
        in_specs=[pl.BlockSpec((tm, k1), lambda j, i: (i, 0)),
                  pl.BlockSpec((tm, k2), lambda j, i: (i, 0)),
                  pl.BlockSpec((None, k1, tn), lambda j, i: (layer, 0, j)),
                  pl.BlockSpec((None, k2, tn), lambda j, i: (layer, 1, j)),
                  pl.BlockSpec((tm, tn), lambda j, i: (i, j))],
        out_specs=pl.BlockSpec((tm, tn), lambda j, i: (i, j)),
        out_shape=jax.ShapeDtypeStruct((m, n), F32),
        scratch_shapes=[pltpu.VMEM((k1, tn), BF16), pltpu.VMEM((k2, tn), BF16)],
        compiler_params=pltpu.CompilerParams(dimension_semantics=("parallel", "arbitrary"),
                                             vmem_limit_bytes=PROJ_VMEM_LIMIT_BYTES),
        name="proj_out",
    )(a1, a2, w, w, res)


def _conv_kernel(up_ref, uc_ref, un_ref, w_ref, b_ref, o_ref, buf_ref, *, ts, scale, transpose):
    t = pl.program_id(2)
    nt = pl.num_programs(2)
    pad = SUBLANES
    buf_ref[0:pad, :] = jnp.where(t > 0, up_ref[0], 0.0)
    buf_ref[pad:pad + ts, :] = uc_ref[0]
    buf_ref[pad + ts:2 * pad + ts, :] = jnp.where(t < nt - 1, un_ref[0], 0.0)
    acc = jnp.zeros((ts, buf_ref.shape[1]), F32) + b_ref[...]
    for j in range(CONV_W):
        acc = acc + buf_ref[pl.ds(pad - CONV_W // 2 + j, ts), :] * w_ref[j:j + 1, :]
    y = acc * jax.nn.sigmoid(acc) * scale
    if transpose:
        o_ref[0] = y.T.astype(o_ref.dtype)
    else:
        o_ref[0] = y.astype(o_ref.dtype)


def _short_conv_silu(u, conv_w, conv_b, col0, width, scale, transpose, ts=512):
    bsz, s, _ = u.shape
    tc = MLSTM_DK
    ts = min(ts, s)
    cb0 = col0 // tc
    nrb = ts // SUBLANES
    last_rb = s // SUBLANES - 1
    kern = functools.partial(_conv_kernel, ts=ts, scale=scale, transpose=transpose)
    if transpose:
        out_spec = pl.BlockSpec((1, tc, ts), lambda b, c, t: (b, c, t))
        out_shape = jax.ShapeDtypeStruct((bsz, width, s), BF16)
    else:
        out_spec = pl.BlockSpec((1, ts, tc), lambda b, c, t: (b, t, c))
        out_shape = jax.ShapeDtypeStruct((bsz, s, width), BF16)
    return pl.pallas_call(
        kern,
        grid=(bsz, width // tc, s // ts),
        in_specs=[pl.BlockSpec((1, SUBLANES, tc),
                               lambda b, c, t: (b, jnp.maximum(t * nrb - 1, 0), cb0 + c)),
                  pl.BlockSpec((1, ts, tc), lambda b, c, t: (b, t, cb0 + c)),
                  pl.BlockSpec((1, SUBLANES, tc),
                               lambda b, c, t: (b, jnp.minimum((t + 1) * nrb, last_rb), cb0 + c)),
                  pl.BlockSpec((CONV_W, tc), lambda b, c, t: (0, cb0 + c)),
                  pl.BlockSpec((1, tc), lambda b, c, t: (0, cb0 + c))],
        out_specs=out_spec,
        out_shape=out_shape,
        scratch_shapes=[pltpu.VMEM((ts + 2 * SUBLANES, tc), F32)],
        compiler_params=_params(("parallel", "parallel", "parallel")),
        name="conv_silu_t" if transpose else "conv_silu",
    )(u, u, u, conv_w, conv_b.reshape(1, -1))


def _log_sigmoid(x):
    return jnp.minimum(x, 0.0) - jnp.log1p(jnp.exp(-jnp.abs(x)))


def _mlstm_kernel(bias_ref, q_ref, kt_ref, v_ref, g_ref, *rest, chunk, nheads, reverse, final):
    if final:
        hprev_ref, og_ref, hg_ref, o_ref, ct_ref, m_ref, vext_ref, wi_ref, col_ref, row_ref = rest
    else:
        o_ref, ct_ref, m_ref, vext_ref, wi_ref, col_ref, row_ref = rest
    c_idx = pl.program_id(1)
    dv, dk = MLSTM_DV, MLSTM_DK
    ln = chunk

    @pl.when(c_idx == 0)
    def _():
        ct_ref[...] = jnp.zeros_like(ct_ref)
        m_ref[...] = jnp.zeros_like(m_ref)

    gi, gf = (2, 3) if reverse else (0, 1)
    row = lax.broadcasted_iota(jnp.int32, (ln, ln), 0)
    col = lax.broadcasted_iota(jnp.int32, (ln, ln), 1)
    cum_mask = ((row >= col) if reverse else (row <= col)).astype(F32)
    tri = (col >= row) if reverse else (col <= row)
    diag = row == col
    ones_cols = jnp.ones((ln, LANES), vext_ref.dtype)

    i_rows = [g_ref[gi, hh] + bias_ref[gi * nheads + hh] for hh in range(nheads)]
    f_rows = [_log_sigmoid(g_ref[gf, hh] + bias_ref[gf * nheads + hh]) for hh in range(nheads)]
    f_all = jnp.concatenate(f_rows + [jnp.zeros((SUBLANES - nheads, ln), F32)], axis=0)
    b_all = jnp.dot(f_all, cum_mask, precision=lax.Precision.HIGHEST, preferred_element_type=F32)
    for hh in range(nheads):
        i_row = i_rows[hh]
        b_row = b_all[hh:hh + 1, :]
        total = b_row[:, 0:1] if reverse else b_row[:, ln - 1:ln]
        b_col = jnp.sum(jnp.where(diag, b_row, 0.0), axis=1, keepdims=True)
        m0 = m_ref[hh, 0:1, 0:1]
        d_log = jnp.where(tri, b_col - b_row + i_row, MASK_VALUE)
        a_col = b_col + m0
        m_t = jnp.maximum(a_col, jnp.max(d_log, axis=1, keepdims=True))
        wi_ref[hh] = jnp.exp(d_log - m_t)
        col_ref[hh, :, 0:1] = jnp.exp(a_col - m_t)
        col_ref[hh, :, 1:2] = jnp.exp(-m_t)
        g_row = total - b_row + i_row
        m_new = jnp.maximum(total + m0, jnp.max(g_row, axis=1, keepdims=True))
        row_ref[hh, 0:1, :] = jnp.exp(g_row - m_new)
        row_ref[hh, 1:2, :] = jnp.broadcast_to(jnp.exp(total + m0 - m_new), (1, ln))
        m_ref[hh] = jnp.broadcast_to(m_new, m_ref.shape[1:])

    for hh in range(nheads):
        w_inter = col_ref[hh, :, 0:1]
        inv_floor = col_ref[hh, :, 1:2]
        w_add = row_ref[hh, 0:1, :]
        decay = row_ref[hh, 1:2, 0:1]
        q = q_ref[0, :, hh * dk:(hh + 1) * dk]
        kt = kt_ref[0, hh * dk:(hh + 1) * dk, :]
        vext_ref[hh, :, 0:dv] = v_ref[0, :, hh * dv:(hh + 1) * dv].astype(vext_ref.dtype)
        vext_ref[hh, :, dv:] = ones_cols
        vext = vext_ref[hh]

        s_qk = jnp.dot(q, kt, preferred_element_type=F32) * wi_ref[hh]
        intra = jnp.dot(s_qk.astype(BF16), vext, preferred_element_type=F32)
        ct = ct_ref[hh]
        inter = jnp.dot(q, ct.astype(BF16), preferred_element_type=F32)
        tot = intra + w_inter * inter
        den = tot[:, dv:dv + 1]
        h = tot[:, 0:dv] / jnp.maximum(jnp.abs(den), inv_floor)

        ktw = (kt.astype(F32) * w_add).astype(BF16)
        ct_ref[hh] = decay * ct + jnp.dot(ktw, vext, preferred_element_type=F32)

        cols = slice(hh * dv, (hh + 1) * dv)
        if final:
            hs = h + hprev_ref[0, :, cols]
            hn = hs * lax.rsqrt(jnp.mean(hs * hs, axis=-1, keepdims=True) + RMS_EPS)
            o_ref[0, :, cols] = (hn * hg_ref[:, cols]
                                 * jax.nn.sigmoid(og_ref[0, :, cols])).astype(o_ref.dtype)
        else:
            o_ref[0, :, cols] = h


def _mlstm_scan(bias, q, kt, z, v_blk, gates_t, reverse, hprev=None, o_blk=None, head_g=None):
    bsz, s, qkw = q.shape
    nheads = qkw // MLSTM_DK
    ln = min(MLSTM_CHUNK, s)
    nc = s // ln
    final = hprev is not None

    def cidx(c):
        return (nc - 1 - c) if reverse else c

    vw = nheads * MLSTM_DV
    wide = lambda blk: pl.BlockSpec((1, ln, vw), lambda b, c: (b, cidx(c), blk))
    in_specs = [pl.BlockSpec(memory_space=pltpu.SMEM),
                pl.BlockSpec((1, ln, qkw), lambda b, c: (b, cidx(c), 0)),
                pl.BlockSpec((1, qkw, ln), lambda b, c: (b, 0, cidx(c))),
                wide(v_blk),
                pl.BlockSpec((None, 4, nheads, 1, ln), lambda b, c: (b, 0, 0, 0, cidx(c)))]
    args = [bias, q, kt, z, gates_t]
    if final:
        in_specs += [wide(0), wide(o_blk), pl.BlockSpec((1, vw), lambda b, c: (0, 0))]
        args += [hprev, z, head_g]
    kern = functools.partial(_mlstm_kernel, chunk=ln, nheads=nheads, reverse=reverse, final=final)
    return pl.pallas_call(
        kern,
        grid=(bsz, nc),
        in_specs=in_specs,
        out_specs=wide(0),
        out_shape=jax.ShapeDtypeStruct((bsz, s, vw), BF16 if final else F32),
        scratch_shapes=[pltpu.VMEM((nheads, MLSTM_DK, MLSTM_DV + LANES), F32),
                        pltpu.VMEM((nheads, SUBLANES, LANES), F32),
                        pltpu.VMEM((nheads, ln, MLSTM_DV + LANES), BF16),
                        pltpu.VMEM((nheads, ln, ln), F32),
                        pltpu.VMEM((nheads, ln, LANES), F32),
                        pltpu.VMEM((nheads, SUBLANES, ln), F32)],
        compiler_params=_params(("parallel", "arbitrary")),
        name="mlstm_fwd" if final else "mlstm_bwd",
    )(*args)


ATTN_TILE = 1024
ATTN_HALO = ATTN_HALF * 16
ATTN_SUB = 128


def _band_bias(nq, nk, slope_d):
    qi = lax.broadcasted_iota(jnp.int32, (nq, nk), 0)
    kj = lax.broadcasted_iota(jnp.int32, (nq, nk), 1)
    rel = jnp.abs(kj - ATTN_HALF - qi)
    return jnp.where(rel <= ATTN_HALF, (-LOG2E * slope_d) * rel.astype(F32), MASK_VALUE)


def _band_scores(qs, ks, bias, key_lo, key_hi):
    s = lax.dot_general(qs.astype(BF16), ks.astype(BF16), (((1,), (1,)), ((), ())),
                        preferred_element_type=F32) + bias
    if key_lo is not None or key_hi is not None:
        kj = lax.broadcasted_iota(jnp.int32, s.shape, 1)
        inside = None
        if key_lo is not None:
            inside = kj >= key_lo
        if key_hi is not None:
            inside = (kj < key_hi) if inside is None else (inside & (kj < key_hi))
        s = jnp.where(inside, s, MASK_VALUE)
    m = jnp.max(s, axis=1, keepdims=True)
    return s, jnp.broadcast_to(m, (s.shape[0], LANES))


def _band_softmax_pv(s, m_b, vs):
    nq, nk = s.shape
    reps = [m_b] * (nk // LANES) + ([m_b[:, :nk % LANES]] if nk % LANES else [])
    m_wide = reps[0] if len(reps) == 1 else jnp.concatenate(reps, axis=1)
    p = jnp.exp2(s - m_wide).astype(BF16)
    v_ext = jnp.concatenate([vs.astype(BF16), jnp.ones((nk, LANES), BF16)], axis=1)
    r = jnp.dot(p, v_ext, preferred_element_type=F32)
    l_b = r[:, ATTN_HEAD_DIM:]
    return r[:, :ATTN_HEAD_DIM] / l_b, m_b * (1.0 / LOG2E) + jnp.log(l_b)


def _attn_kernel(slopes_ref, q_ref, kp_ref, kc_ref, kn_ref, vp_ref, vc_ref, vn_ref, g_ref, y_ref,
                 q4, k4, v4, acc1, lse1, acc2, lse2, acc3, lse3, ynat, sbuf, mbuf, *, tq):
    t = pl.program_id(1)
    half = ATTN_HALF
    dh = ATTN_HEAD_DIM
    halo = ATTN_HALO
    n4, h4 = tq // 4, halo // 4
    n16 = tq // 16
    sub = ATTN_SUB
    scale = LOG2E * ATTN_HEAD_DIM ** -0.5
    first_tile = t == 0
    last_tile = t == pl.num_programs(1) - 1
    slope = slopes_ref[pl.program_id(2)]

    for c in range(4):
        q4[c] = q_ref[0, pl.ds(c, n4, stride=4), :] * scale
        for dst, (p_ref, c_ref, n_ref) in ((k4, (kp_ref, kc_ref, kn_ref)),
                                           (v4, (vp_ref, vc_ref, vn_ref))):
            dst[c, 0:h4, :] = p_ref[0, pl.ds(c, h4, stride=4), :]
            dst[c, h4:h4 + n4, :] = c_ref[0, pl.ds(c, n4, stride=4), :]
            dst[c, h4 + n4:, :] = n_ref[0, pl.ds(c, h4, stride=4), :]

    def bounds(j, nblk, nsub):
        lo = jnp.where(first_tile, half, 0) if j == 0 else None
        hi = jnp.where(last_tile, nsub + half, nsub + 2 * half) if j == nblk - 1 else None
        return lo, hi

    def run_blocks(blocks, nsub, slope_d):
        nk = nsub + 2 * half
        bias = _band_bias(nsub, nk, slope_d)
        for i, (load_q, load_k, _, lo, hi, _) in enumerate(blocks):
            s, m_b = _band_scores(load_q(), load_k(), bias, lo, hi)
            sbuf[i, 0:nsub, 0:nk] = s
            mbuf[i, 0:nsub, :] = m_b
        for i, (_, _, load_v, _, _, store) in enumerate(blocks):
            store(*_band_softmax_pv(sbuf[i, 0:nsub, 0:nk], mbuf[i, 0:nsub, :], load_v()))

    def window(refs, r0, nk):
        p_ref, c_ref, n_ref = refs
        parts = []
        if r0 < 0:
            parts.append(p_ref[0, halo + r0:halo, :])
        parts.append(c_ref[0, max(r0, 0):min(r0 + nk, tq), :])
        if r0 + nk > tq:
            parts.append(n_ref[0, 0:r0 + nk - tq, :])
        return parts[0] if len(parts) == 1 else jnp.concatenate(parts, axis=0)

    def store_to(acc, lse, *idx):
        def store(o, lse_b):
            acc[idx] = o
            lse[idx] = lse_b
        return store

    krefs, vrefs = (kp_ref, kc_ref, kn_ref), (vp_ref, vc_ref, vn_ref)
    nk = sub + 2 * half
    nblk = tq // sub
    blocks = []
    for j in range(nblk):
        rows = slice(j * sub, (j + 1) * sub)
        blocks.append((lambda rows=rows: q_ref[0, rows, :] * scale,
                       lambda j=j: window(krefs, j * sub - half, nk),
                       lambda j=j: window(vrefs, j * sub - half, nk),
                       *bounds(j, nblk, sub), store_to(acc1, lse1, rows, slice(None))))
    run_blocks(blocks, sub, slope)

    nblk = n4 // sub
    blocks = []
    for c in range(4):
        for j in range(nblk):
            rows = slice(j * sub, (j + 1) * sub)
            krows = slice(h4 + j * sub - half, h4 + j * sub - half + nk)
            blocks.append((lambda c=c, rows=rows: q4[c, rows, :],
                           lambda c=c, krows=krows: k4[c, krows, :],
                           lambda c=c, krows=krows: v4[c, krows, :],
                           *bounds(j, nblk, sub), store_to(acc2, lse2, c, rows, slice(None))))
    run_blocks(blocks, sub, slope * 4)

    nsub = min(sub, n16)
    nk = nsub + 2 * half
    nblk = n16 // nsub
    blocks = []
    for c in range(4):
        for c16 in range(4):
            for j in range(nblk):
                rows_q = pl.ds(c16 + 4 * j * nsub, nsub, stride=4)
                rows_k = pl.ds(h4 + c16 + 4 * (j * nsub - half), nk, stride=4)
                blocks.append((lambda c=c, rows_q=rows_q: q4[c, rows_q, :],
                               lambda c=c, rows_k=rows_k: k4[c, rows_k, :],
                               lambda c=c, rows_k=rows_k: v4[c, rows_k, :],
                               *bounds(j, nblk, nsub),
                               store_to(acc3, lse3, c, rows_q, slice(None))))
    run_blocks(blocks, nsub, slope * 16)

    for c in range(4):
        rows = pl.ds(c, n4, stride=4)
        l1, l2, l3 = lse1[rows, :], lse2[c], lse3[c]
        mx = jnp.maximum(jnp.maximum(l1, l2), l3)
        e1, e2, e3 = jnp.exp(l1 - mx), jnp.exp(l2 - mx), jnp.exp(l3 - mx)
        o = (e1 * acc1[rows, :] + e2 * acc2[c] + e3 * acc3[c]) / (e1 + e2 + e3)
        on = o * lax.rsqrt(jnp.mean(o * o, axis=-1, keepdims=True) + RMS_EPS)
        ynat[rows, :] = on * g_ref[...]
    y_ref[0] = ynat[...].astype(y_ref.dtype)


def _dilated_attention(slopes, qkv, head_g):
    bsz, s, width3 = qkv.shape
    width = width3 // 3
    tq = min(ATTN_TILE, s)
    halo = ATTN_HALO
    assert tq % halo == 0 and s % tq == 0 and (tq // 16) % min(ATTN_SUB, tq // 16) == 0
    dh = ATTN_HEAD_DIM
    nh = width // dh
    nhb = tq // halo
    last_hb = s // halo - 1
    n4, h4 = tq // 4, halo // 4
    cur = lambda o: pl.BlockSpec((1, tq, dh), lambda b, t, c, sl: (b, t, o + c))
    prev = lambda o: pl.BlockSpec((1, halo, dh),
                                  lambda b, t, c, sl: (b, jnp.maximum(t * nhb - 1, 0), o + c))
    nxt = lambda o: pl.BlockSpec((1, halo, dh),
                                 lambda b, t, c, sl: (b, jnp.minimum((t + 1) * nhb, last_hb), o + c))
    kern = functools.partial(_attn_kernel, tq=tq)
    nstage = max(tq // ATTN_SUB, 16 * ((tq // 16) // min(ATTN_SUB, tq // 16)))
    win = pltpu.VMEM((4, n4 + 2 * h4, dh), F32)
    cls = pltpu.VMEM((4, n4, dh), F32)
    nat = pltpu.VMEM((tq, dh), F32)
    return pl.pallas_call(
        kern,
        grid_spec=pltpu.PrefetchScalarGridSpec(
            num_scalar_prefetch=1,
            grid=(bsz, s // tq, width // dh),
            in_specs=[cur(0), prev(nh), cur(nh), nxt(nh), prev(2 * nh), cur(2 * nh), nxt(2 * nh),
                      pl.BlockSpec((1, dh), lambda b, t, c, sl: (0, c))],
            out_specs=pl.BlockSpec((1, tq, dh), lambda b, t, c, sl: (b, t, c)),
            scratch_shapes=[cls, win, win, nat, nat, cls, cls, cls, cls, nat,
                            pltpu.VMEM((nstage, ATTN_SUB, ATTN_SUB + 2 * ATTN_HALF), F32),
                            pltpu.VMEM((nstage, ATTN_SUB, LANES), F32)]),
        out_shape=jax.ShapeDtypeStruct((bsz, s, width), BF16),
        compiler_params=_params(("parallel", "parallel", "parallel")),
        name="dilated_attn",
    )(slopes, qkv, qkv, qkv, qkv, qkv, qkv, qkv, head_g.reshape(1, width))


def _start_row_gather(idx_ref, src_hbm, dst, sem, nrows):
    def body(r, carry):
        pltpu.make_async_copy(src_hbm.at[pl.ds(idx_ref[0, 0, r], 1), :],
                              dst.at[pl.ds(r, 1), :], sem).start()
        return carry

    lax.fori_loop(0, nrows, body, 0, unroll=8)


def _wait_row_gather(src_hbm, dst, sem, nrows):
    pltpu.make_async_copy(src_hbm.at[pl.ds(0, nrows), :], dst, sem).wait()


def _expert_up_kernel(be_ref, nb_ref, idx_ref, idx_next_ref, h_hbm, wg_ref, wu_ref, o_ref,
                      xbuf, sem):
    i = pl.program_id(0)
    n_used = nb_ref[0]
    slot = i % 2
    blk = xbuf.shape[1]

    @pl.when(i == 0)
    def _():
        _start_row_gather(idx_ref, h_hbm, xbuf.at[0], sem.at[0], blk)

    @pl.when(i + 1 < n_used)
    def _():
        _start_row_gather(idx_next_ref, h_hbm, xbuf.at[1 - slot], sem.at[1 - slot], blk)

    @pl.when(i < n_used)
    def _():
        _wait_row_gather(h_hbm, xbuf.at[slot], sem.at[slot], blk)
        x = xbuf[slot].astype(BF16)
        g = jnp.dot(x, wg_ref[...], preferred_element_type=F32)
        u = jnp.dot(x, wu_ref[...], preferred_element_type=F32)
        o_ref[...] = (g * jax.nn.sigmoid(g) * u).astype(o_ref.dtype)

    @pl.when(i >= n_used)
    def _():
        o_ref[...] = jnp.zeros_like(o_ref)


HI16 = 0xFFFF0000


def _pack_bf16_pairs(lo, hi):
    bits = lambda v: lax.bitcast_convert_type(v.astype(BF16).astype(F32), jnp.uint32)
    return (bits(hi) & jnp.uint32(HI16)) | (bits(lo) >> 16)


def _unpack_bf16_pairs(p):
    lo = lax.bitcast_convert_type(p << 16, F32)
    hi = lax.bitcast_convert_type(p & jnp.uint32(HI16), F32)
    return lo, hi


def _expert_down_kernel(be_ref, nb_ref, h_ref, wd_ref, o_ref):
    i = pl.program_id(0)
    half = o_ref.shape[1]

    @pl.when(i < nb_ref[0])
    def _():
        y = jnp.dot(h_ref[...], wd_ref[...], preferred_element_type=F32)
        o_ref[...] = _pack_bf16_pairs(y[:, :half], y[:, half:])

    @pl.when(i >= nb_ref[0])
    def _():
        o_ref[...] = jnp.zeros_like(o_ref)


def _expert_ffn(h, slot_tok, block_e, n_used, w_gate, w_up, w_down, layer):
    d = h.shape[1]
    p = slot_tok.shape[0]
    ff = w_gate.shape[3]
    blk = EXPERT_BLOCK
    nblk = p // blk
    idx = slot_tok.reshape(nblk, 1, blk)
    h1 = pl.pallas_call(
        _expert_up_kernel,
        grid_spec=pltpu.PrefetchScalarGridSpec(
            num_scalar_prefetch=2,
            grid=(nblk,),
            in_specs=[pl.BlockSpec((1, 1, blk), lambda i, be, nb: (i, 0, 0),
                                   memory_space=pltpu.SMEM),
                      pl.BlockSpec((1, 1, blk), lambda i, be, nb: (jnp.minimum(i + 1, nblk - 1), 0, 0),
                                   memory_space=pltpu.SMEM),
                      pl.BlockSpec(memory_space=pl.ANY),
                      pl.BlockSpec((None, None, d, ff), lambda i, be, nb: (layer, be[i], 0, 0)),
                      pl.BlockSpec((None, None, d, ff), lambda i, be, nb: (layer, be[i], 0, 0))],
            out_specs=pl.BlockSpec((blk, ff), lambda i, be, nb: (i, 0)),
            scratch_shapes=[pltpu.VMEM((2, blk, d), F32), pltpu.SemaphoreType.DMA((2,))]),
        out_shape=jax.ShapeDtypeStruct((p, ff), BF16),
        compiler_params=_params(("arbitrary",)),
        name="expert_up",
    )(block_e, n_used, idx, idx, h, w_gate, w_up)
    return pl.pallas_call(
        _expert_down_kernel,
        grid_spec=pltpu.PrefetchScalarGridSpec(
            num_scalar_prefetch=2,
            grid=(nblk,),
            in_specs=[pl.BlockSpec((blk, ff), lambda i, be, nb: (i, 0)),
                      pl.BlockSpec((None, None, ff, d), lambda i, be, nb: (layer, be[i], 0, 0))],
            out_specs=pl.BlockSpec((blk, d // 2), lambda i, be, nb: (i, 0))),
        out_shape=jax.ShapeDtypeStruct((p, d // 2), jnp.uint32),
        compiler_params=_params(("arbitrary",)),
        name="expert_down",
    )(block_e, n_used, h1, w_down)


def _combine_kernel(p0_ref, p1_ref, p0n_ref, p1n_ref, x_ref, w0_ref, w1_ref, g_ref, y_hbm,
                    *rest, emit_x):
    if emit_x:
        xo_ref, ho_ref, buf, sem = rest
    else:
        ho_ref, buf, sem = rest
    i = pl.program_id(0)
    nt = pl.num_programs(0)
    slot = i % 2
    tm = buf.shape[2]

    @pl.when(i == 0)
    def _():
        _start_row_gather(p0_ref, y_hbm, buf.at[0, 0], sem.at[0], tm)
        _start_row_gather(p1_ref, y_hbm, buf.at[0, 1], sem.at[0], tm)

    @pl.when(i + 1 < nt)
    def _():
        _start_row_gather(p0n_ref, y_hbm, buf.at[1 - slot, 0], sem.at[1 - slot], tm)
        _start_row_gather(p1n_ref, y_hbm, buf.at[1 - slot, 1], sem.at[1 - slot], tm)

    _wait_row_gather(y_hbm, buf.at[slot, 0], sem.at[slot], tm)
    _wait_row_gather(y_hbm, buf.at[slot, 1], sem.at[slot], tm)
    half = buf.shape[3]
    lo0, hi0 = _unpack_bf16_pairs(buf[slot, 0])
    lo1, hi1 = _unpack_bf16_pairs(buf[slot, 1])
    w0, w1 = w0_ref[...], w1_ref[...]
    x_lo = x_ref[:, :half] + w0 * lo0 + w1 * lo1
    x_hi = x_ref[:, half:] + w0 * hi0 + w1 * hi1
    if emit_x:
        xo_ref[:, :half] = x_lo
        xo_ref[:, half:] = x_hi
    ssq = (jnp.sum(x_lo * x_lo, axis=-1, keepdims=True)
           + jnp.sum(x_hi * x_hi, axis=-1, keepdims=True))
    inv = lax.rsqrt(ssq * (1.0 / (2 * half)) + RMS_EPS)
    ho_ref[:, :half] = (x_lo * inv * g_ref[:, :half]).astype(ho_ref.dtype)
    ho_ref[:, half:] = (x_hi * inv * g_ref[:, half:]).astype(ho_ref.dtype)


def _combine_norm(x2d, y, pos, wts, g, emit_x, tm=256):
    t, d = x2d.shape
    nt = t // tm
    pidx = pos.reshape(EXPERT_TOPK, nt, 1, tm)
    wcol = wts.reshape(EXPERT_TOPK, t, 1)
    cur = lambda i: (i, 0, 0)
    nxt = lambda i: (jnp.minimum(i + 1, nt - 1), 0, 0)
    ispec = lambda f: pl.BlockSpec((1, 1, tm), f, memory_space=pltpu.SMEM)
    row = pl.BlockSpec((tm, d), lambda i: (i, 0))
    wspec = pl.BlockSpec((tm, 1), lambda i: (i, 0))
    if emit_x:
        out_specs = [row, row]
        out_shape = [jax.ShapeDtypeStruct((t, d), F32), jax.ShapeDtypeStruct((t, d), BF16)]
    else:
        out_specs = row
        out_shape = jax.ShapeDtypeStruct((t, d), F32)
    return pl.pallas_call(
        functools.partial(_combine_kernel, emit_x=emit_x),
        grid=(nt,),
        in_specs=[ispec(cur), ispec(cur), ispec(nxt), ispec(nxt), row, wspec, wspec,
                  pl.BlockSpec((1, d), lambda i: (0, 0)),
                  pl.BlockSpec(memory_space=pl.ANY)],
        out_specs=out_specs,
        out_shape=out_shape,
        scratch_shapes=[pltpu.VMEM((2, EXPERT_TOPK, tm, d // 2), jnp.uint32),
                        pltpu.SemaphoreType.DMA((2,))],
        compiler_params=_params(("arbitrary",)),
        name="moe_combine_norm",
    )(pidx[0], pidx[1], pidx[0], pidx[1], x2d, wcol[0], wcol[1], g.reshape(1, d), y)


def _route(eid, n_experts):
    t = eid.shape[1]
    a = EXPERT_TOPK * t
    flat_e = eid.reshape(a)
    onehot = (flat_e[:, None] == jnp.arange(n_experts, dtype=jnp.int32)[None, :]).astype(jnp.int32)
    csum = jnp.cumsum(onehot, axis=0)
    counts = csum[-1]
    blocks = (counts + EXPERT_BLOCK - 1) // EXPERT_BLOCK
    blk_end = jnp.cumsum(blocks)
    pad_start = (blk_end - blocks) * EXPERT_BLOCK
    dest = jnp.sum(onehot * (csum - 1 + pad_start[None, :]), axis=1)
    n_blocks = -(-a // EXPERT_BLOCK) + n_experts
    p = n_blocks * EXPERT_BLOCK
    tok = jnp.arange(a, dtype=jnp.int32) % t
    slot_tok = jnp.zeros((p,), jnp.int32).at[dest].set(tok)
    block_e = jnp.sum((blk_end[None, :] <= jnp.arange(n_blocks, dtype=jnp.int32)[:, None])
                      .astype(jnp.int32), axis=1)
    block_e = jnp.minimum(block_e, n_experts - 1)
    n_used = blk_end[-1:].astype(jnp.int32)
    return slot_tok, block_e, n_used, dest.reshape(EXPERT_TOPK, t)


def kernel(x, norm1_g, w_in, b_gate, conv_w, conv_b, head_norm_g, w_out, norm2_g, w_router_group,
           b_router_group, w_router_expert, b_router_expert, w_gate, w_up, w_down, final_norm_g):
    depth = norm1_g.shape[0]
    bsz, s, d = x.shape
    t = bsz * s
    mw = d // 2
    nmh = mw // MLSTM_DV
    qkw = nmh * MLSTM_DK
    aw = d - mw
    nah = aw // ATTN_HEAD_DIM
    ngate = 4 * nmh
    n_groups = w_router_group.shape[2]
    n_experts = w_router_expert.shape[2]

    c_v = 2 * qkw
    c_o = c_v + mw
    c_g = c_o + mw
    c_a = c_g + ngate
    assert ngate <= LANES and c_a + 3 * aw == w_in.shape[2]
    w_in_t = jnp.swapaxes(w_in, 1, 2)
    npad = ROUTER_ROWS - n_groups - n_experts
    w_router_t = jnp.pad(jnp.concatenate([w_router_group, w_router_expert], axis=2)
                         .transpose(0, 2, 1), ((0, 0), (0, npad), (0, 0)))
    b_router_t = jnp.pad(jnp.concatenate([b_router_group, b_router_expert], axis=1),
                         ((0, 0), (0, npad))).reshape(depth, ROUTER_ROWS, 1)
    wg_b, wu_b, wd_b = w_gate.astype(BF16), w_up.astype(BF16), w_down.astype(BF16)
    slopes = jnp.exp2(-8.0 * jnp.arange(1, nah + 1, dtype=F32) / nah)

    x2d = x.reshape(t, d)
    h = _rmsnorm(x2d, norm1_g[0], BF16)
    for l in range(depth):
        head_g = head_norm_g[l]
        z_m = _matmul(h, w_in_t, l, F32, row0=0, n=c_g).reshape(bsz, s, c_g)
        gates = _matmul(h, w_in_t, l, F32, row0=c_g, n=LANES)[:, :ngate]
        z_a = _matmul(h, w_in_t, l, F32, row0=c_a, n=3 * aw).reshape(bsz, s, 3 * aw)

        q_m = _short_conv_silu(z_m, conv_w[l], conv_b[l], 0, qkw, MLSTM_DK ** -0.5,
                               transpose=False)
        kt_m = _short_conv_silu(z_m, conv_w[l], conv_b[l], qkw, qkw, 1.0, transpose=True)
        gates_t = gates.reshape(bsz, s, 4, nmh).transpose(0, 2, 3, 1).reshape(bsz, 4, nmh, 1, s)
        h_bwd = _mlstm_scan(b_gate[l], q_m, kt_m, z_m, 1, gates_t, reverse=True)
        y_m = _mlstm_scan(b_gate[l], q_m, kt_m, z_m, 1, gates_t, reverse=False, hprev=h_bwd,
                          o_blk=2, head_g=head_g[:mw].reshape(1, mw))

        y_a = _dilated_attention(slopes, z_a, head_g[mw:])

        x2d = _out_proj(y_m.reshape(t, mw), y_a.reshape(t, aw), w_out, l, x2d)

        h2, eid, wts = _rmsnorm_router(x2d, norm2_g[l], w_router_t[l], b_router_t[l],
                                       n_groups, n_experts // n_groups)
        slot_tok, block_e, n_used, pos = _route(eid[:EXPERT_TOPK], n_experts)
        yb = _expert_ffn(h2, slot_tok, block_e, n_used, wg_b, wu_b, wd_b, l)
        if l + 1 < depth:
            x2d, h = _combine_norm(x2d, yb, pos, wts[:EXPERT_TOPK], norm1_g[l + 1], emit_x=True)
        else:
            out = _combine_norm(x2d, yb, pos, wts[:EXPERT_TOPK], final_norm_g, emit_x=False)
    return out.reshape(bsz, s, d)
```

```python
import functools

import jax
import jax.numpy as jnp
from jax import lax
from jax.experimental import pallas as pl
from jax.experimental.pallas import tpu as pltpu

MLSTM_DV = 512
MLSTM_DK = 256
ATTN_HEAD_DIM = 128
CONV_W = 5
DILATED_PATTERNS = ((128, 1), (512, 4), (2048, 16))
ATTN_HALF = 64
EXPERT_TOPK = 2
RMS_EPS = 1e-6
MASK_VALUE = -1e30
LOG2E = 1.4426950408889634

LANES = 128
SUBLANES = 8
VMEM_LIMIT_BYTES = 56 * 1024 * 1024
PROJ_VMEM_LIMIT_BYTES = 60 * 1024 * 1024

MLSTM_CHUNK = 256
EXPERT_BLOCK = 256
ROUTER_ROWS = 32

BF16 = jnp.bfloat16
F32 = jnp.float32


def _params(semantics):
    return pltpu.CompilerParams(dimension_semantics=semantics, vmem_limit_bytes=VMEM_LIMIT_BYTES)


def _rmsnorm_kernel(x_ref, g_ref, o_ref):
    x = x_ref[...]
    ms = jnp.mean(x * x, axis=-1, keepdims=True)
    o_ref[...] = (x * lax.rsqrt(ms + RMS_EPS) * g_ref[...]).astype(o_ref.dtype)


def _rmsnorm(x2d, g, out_dtype, tm=256):
    t, d = x2d.shape
    return pl.pallas_call(
        _rmsnorm_kernel,
        grid=(t // tm,),
        in_specs=[pl.BlockSpec((tm, d), lambda i: (i, 0)),
                  pl.BlockSpec((1, d), lambda i: (0, 0))],
        out_specs=pl.BlockSpec((tm, d), lambda i: (i, 0)),
        out_shape=jax.ShapeDtypeStruct((t, d), out_dtype),
        compiler_params=_params(("parallel",)),
        name="rmsnorm",
    )(x2d, g.reshape(1, d))


def _first_argmax(vals):
    best = vals[0]
    idx = jnp.zeros(best.shape, jnp.int32)
    for i in range(1, len(vals)):
        gt = vals[i] > best
        best = jnp.where(gt, vals[i], best)
        idx = jnp.where(gt, i, idx)
    return best, idx


def _softmax_list(vals):
    mx = functools.reduce(jnp.maximum, vals)
    es = [jnp.exp(v - mx) for v in vals]
    tot = functools.reduce(lambda a, b: a + b, es)
    return [e / tot for e in es]


def _rmsnorm_router_kernel(x_ref, g_ref, wrt_ref, brt_ref, h_ref, eid_ref, wt_ref,
                           *, n_groups, per_group):
    x = x_ref[...]
    ms = jnp.mean(x * x, axis=-1, keepdims=True)
    h = x * lax.rsqrt(ms + RMS_EPS) * g_ref[...]
    h_ref[...] = h.astype(h_ref.dtype)
    def split(v):
        hi = v.astype(BF16)
        return hi, (v - hi.astype(F32)).astype(BF16)

    def dot_nt(a, b):
        return lax.dot_general(a, b, (((1,), (1,)), ((), ())), preferred_element_type=F32)

    w_hi, w_lo = split(wrt_ref[...])
    h_hi, h_lo = split(h)
    lt = dot_nt(w_hi, h_hi) + dot_nt(w_lo, h_hi) + dot_nt(w_hi, h_lo) + brt_ref[...]
    g_prob = _softmax_list([lt[i:i + 1, :] for i in range(n_groups)])
    g_top_p, g_top = _first_argmax(g_prob)
    e_sel = []
    for j in range(per_group):
        row = lt[n_groups + j:n_groups + j + 1, :]
        for grp in range(1, n_groups):
            r = n_groups + grp * per_group + j
            row = jnp.where(g_top == grp, lt[r:r + 1, :], row)
        e_sel.append(row)
    e_prob = _softmax_list(e_sel)
    p1, i1 = _first_argmax(e_prob)
    p2, i2 = _first_argmax([jnp.where(i1 == j, -1.0, e_prob[j]) for j in range(per_group)])
    denom = p1 + p2
    w1 = g_top_p * p1 / denom
    w2 = g_top_p * p2 / denom
    e1 = g_top * per_group + i1
    e2 = g_top * per_group + i2
    rows = lax.broadcasted_iota(jnp.int32, eid_ref.shape, 0)
    eid_ref[...] = jnp.where(rows == 0, e1, jnp.where(rows == 1, e2, 0))
    wt_ref[...] = jnp.where(rows == 0, w1, jnp.where(rows == 1, w2, 0.0))


def _rmsnorm_router(x2d, g, w_router_t, b_router_t, n_groups, per_group, tm=256):
    t, d = x2d.shape
    kern = functools.partial(_rmsnorm_router_kernel, n_groups=n_groups, per_group=per_group)
    return pl.pallas_call(
        kern,
        grid=(t // tm,),
        in_specs=[pl.BlockSpec((tm, d), lambda i: (i, 0)),
                  pl.BlockSpec((1, d), lambda i: (0, 0)),
                  pl.BlockSpec((ROUTER_ROWS, d), lambda i: (0, 0)),
                  pl.BlockSpec((ROUTER_ROWS, 1), lambda i: (0, 0))],
        out_specs=[pl.BlockSpec((tm, d), lambda i: (i, 0)),
                   pl.BlockSpec((SUBLANES, tm), lambda i: (0, i)),
                   pl.BlockSpec((SUBLANES, tm), lambda i: (0, i))],
        out_shape=[jax.ShapeDtypeStruct((t, d), F32),
                   jax.ShapeDtypeStruct((SUBLANES, t), jnp.int32),
                   jax.ShapeDtypeStruct((SUBLANES, t), F32)],
        compiler_params=_params(("parallel",)),
        name="rmsnorm_router",
    )(x2d, g.reshape(1, d), w_router_t, b_router_t)


WEIGHT_CAST_ROWS = 256


def _matmul_kernel(a_ref, bt_ref, o_ref, bb_ref):
    @pl.when(pl.program_id(1) == 0)
    def _():
        step = min(WEIGHT_CAST_ROWS, bb_ref.shape[0])

        def body(c, carry):
            rows = pl.ds(pl.multiple_of(c * step, step), step)
            bb_ref[rows, :] = bt_ref[0, rows, :].astype(BF16)
            return carry

        lax.fori_loop(0, bb_ref.shape[0] // step, body, 0)

    o_ref[...] = lax.dot_general(a_ref[...], bb_ref[...], (((1,), (1,)), ((), ())),
                                 preferred_element_type=F32).astype(o_ref.dtype)


def _matmul(a, bt, layer, out_dtype, row0, n, tm=512, tn=1024):
    m, k = a.shape
    tn = min(tn, n)
    tm = min(tm, m)
    assert row0 % SUBLANES == 0 and n % tn == 0 and m % tm == 0
    return pl.pallas_call(
        _matmul_kernel,
        grid=(n // tn, m // tm),
        in_specs=[pl.BlockSpec((tm, k), lambda j, i: (i, 0)),
                  pl.BlockSpec((pl.Element(1), pl.Element(tn), pl.Element(k)),
                               lambda j, i: (layer, pl.multiple_of(row0 + j * tn, SUBLANES), 0))],
        out_specs=pl.BlockSpec((tm, tn), lambda j, i: (i, j)),
        out_shape=jax.ShapeDtypeStruct((m, n), out_dtype),
        scratch_shapes=[pltpu.VMEM((tn, k), BF16)],
        compiler_params=pltpu.CompilerParams(dimension_semantics=("parallel", "arbitrary"),
                                             vmem_limit_bytes=PROJ_VMEM_LIMIT_BYTES),
        name="proj_in",
    )(a, bt)


def _out_proj_kernel(a1_ref, a2_ref, b1_ref, b2_ref, r_ref, o_ref, bb1_ref, bb2_ref):
    @pl.when(pl.program_id(1) == 0)
    def _():
        bb1_ref[...] = b1_ref[...].astype(BF16)
        bb2_ref[...] = b2_ref[...].astype(BF16)

    acc = jnp.dot(a1_ref[...], bb1_ref[...], preferred_element_type=F32)
    acc = acc + jnp.dot(a2_ref[...], bb2_ref[...], preferred_element_type=F32)
    o_ref[...] = r_ref[...] + acc


def _out_proj(a1, a2, w, layer, res, tm=512, tn=1024):
    m, k1 = a1.shape
    k2 = a2.shape[1]
    n = w.shape[2]
    tn = min(tn, n)
    tm = min(tm, m)
    assert k1 == k2
    return pl.pallas_call(
        _out_proj_kernel,
        grid=(n // tn, m // tm),
        in_specs=[pl.BlockSpec((tm, k1), lambda j, i: (i, 0)),
                  pl.BlockSpec((tm, k2), lambda j, i: (i, 0)),
                  pl.BlockSpec((None, k1, tn), lambda j, i: (layer, 0, j)),
                  pl.BlockSpec((None, k2, tn), lambda j, i: (layer, 1, j)),
                  pl.BlockSpec((tm, tn), lambda j, i: (i, j))],
        out_specs=pl.BlockSpec((tm, tn), lambda j, i: (i, j)),
        out_shape=jax.ShapeDtypeStruct((m, n), F32),
        scratch_shapes=[pltpu.VMEM((k1, tn), BF16), pltpu.VMEM((k2, tn), BF16)],
        compiler_params=pltpu.CompilerParams(dimension_semantics=("parallel", "arbitrary"),
                                             vmem_limit_bytes=PROJ_VMEM_LIMIT_BYTES),
        name="proj_out",
    )(a1, a2, w, w, res)


def _conv_kernel(up_ref, uc_ref, un_ref, w_ref, b_ref, o_ref, buf_ref, *, ts, scale, transpose):
    t = pl.program_id(2)
    nt = pl.num_programs(2)
    pad = SUBLANES
    buf_ref[0:pad, :] = jnp.where(t > 0, up_ref[0], 0.0)
    buf_ref[pad:pad + ts, :] = uc_ref[0]
    buf_ref[pad + ts:2 * pad + ts, :] = jnp.where(t < nt - 1, un_ref[0], 0.0)
    acc = jnp.zeros((ts, buf_ref.shape[1]), F32) + b_ref[...]
    for j in range(CONV_W):
        acc = acc + buf_ref[pl.ds(pad - CONV_W // 2 + j, ts), :] * w_ref[j:j + 1, :]
    y = acc * jax.nn.sigmoid(acc) * scale
    if transpose:
        o_ref[0] = y.T.astype(o_ref.dtype)
    else:
        o_ref[0] = y.astype(o_ref.dtype)


def _short_conv_silu(u, conv_w, conv_b, col0, width, scale, transpose, ts=512):
    bsz, s, _ = u.shape
    tc = MLSTM_DK
    ts = min(ts, s)
    cb0 = col0 // tc
    nrb = ts // SUBLANES
    last_rb = s // SUBLANES - 1
    kern = functools.partial(_conv_kernel, ts=ts, scale=scale, transpose=transpose)
    if transpose:
        out_spec = pl.BlockSpec((1, tc, ts), lambda b, c, t: (b, c, t))
        out_shape = jax.ShapeDtypeStruct((bsz, width, s), BF16)
    else:
        out_spec = pl.BlockSpec((1, ts, tc), lambda b, c, t: (b, t, c))
        out_shape = jax.ShapeDtypeStruct((bsz, s, width), BF16)
    return pl.pallas_call(
        kern,
        grid=(bsz, width // tc, s // ts),
        in_specs=[pl.BlockSpec((1, SUBLANES, tc),
                               lambda b, c, t: (b, jnp.maximum(t * nrb - 1, 0), cb0 + c)),
                  pl.BlockSpec((1, ts, tc), lambda b, c, t: (b, t, cb0 + c)),
                  pl.BlockSpec((1, SUBLANES, tc),
                               lambda b, c, t: (b, jnp.minimum((t + 1) * nrb, last_rb), cb0 + c)),
                  pl.BlockSpec((CONV_W, tc), lambda b, c, t: (0, cb0 + c)),
                  pl.BlockSpec((1, tc), lambda b, c, t: (0, cb0 + c))],
        out_specs=out_spec,
        out_shape=out_shape,
        scratch_shapes=[pltpu.VMEM((ts + 2 * SUBLANES, tc), F32)],
        compiler_params=_params(("parallel", "parallel", "parallel")),
        name="conv_silu_t" if transpose else "conv_silu",
    )(u, u, u, conv_w, conv_b.reshape(1, -1))


def _log_sigmoid(x):
    return jnp.minimum(x, 0.0) - jnp.log1p(jnp.exp(-jnp.abs(x)))


def _mlstm_kernel(bias_ref, q_ref, kt_ref, v_ref, g_ref, *rest, chunk, nheads, reverse, final):
    if final:
        hprev_ref, og_ref, hg_ref, o_ref, ct_ref, m_ref, vext_ref, wi_ref, col_ref, row_ref = rest
    else:
        o_ref, ct_ref, m_ref, vext_ref, wi_ref, col_ref, row_ref = rest
    c_idx = pl.program_id(1)
    dv, dk = MLSTM_DV, MLSTM_DK
    ln = chunk

    @pl.when(c_idx == 0)
    def _():
        ct_ref[...] = jnp.zeros_like(ct_ref)
        m_ref[...] = jnp.zeros_like(m_ref)

    gi, gf = (2, 3) if reverse else (0, 1)
    row = lax.broadcasted_iota(jnp.int32, (ln, ln), 0)
    col = lax.broadcasted_iota(jnp.int32, (ln, ln), 1)
    cum_mask = ((row >= col) if reverse else (row <= col)).astype(F32)
    tri = (col >= row) if reverse else (col <= row)
    diag = row == col
    ones_cols = jnp.ones((ln, LANES), vext_ref.dtype)

    i_rows = [g_ref[gi, hh] + bias_ref[gi * nheads + hh] for hh in range(nheads)]
    f_rows = [_log_sigmoid(g_ref[gf, hh] + bias_ref[gf * nheads + hh]) for hh in range(nheads)]
    f_all = jnp.concatenate(f_rows + [jnp.zeros((SUBLANES - nheads, ln), F32)], axis=0)
    b_all = jnp.dot(f_all, cum_mask, precision=lax.Precision.HIGHEST, preferred_element_type=F32)
    for hh in range(nheads):
        i_row = i_rows[hh]
        b_row = b_all[hh:hh + 1, :]
        total = b_row[:, 0:1] if reverse else b_row[:, ln - 1:ln]
        b_col = jnp.sum(jnp.where(diag, b_row, 0.0), axis=1, keepdims=True)
        m0 = m_ref[hh, 0:1, 0:1]
        d_log = jnp.where(tri, b_col - b_row + i_row, MASK_VALUE)
        a_col = b_col + m0
        m_t = jnp.maximum(a_col, jnp.max(d_log, axis=1, keepdims=True))
        wi_ref[hh] = jnp.exp(d_log - m_t)
        col_ref[hh, :, 0:1] = jnp.exp(a_col - m_t)
        col_ref[hh, :, 1:2] = jnp.exp(-m_t)
        g_row = total - b_row + i_row
        m_new = jnp.maximum(total + m0, jnp.max(g_row, axis=1, keepdims=True))
        row_ref[hh, 0:1, :] = jnp.exp(g_row - m_new)
        row_ref[hh, 1:2, :] = jnp.broadcast_to(jnp.exp(total + m0 - m_new), (1, ln))
        m_ref[hh] = jnp.broadcast_to(m_new, m_ref.shape[1:])

    for hh in range(nheads):
        w_inter = col_ref[hh, :, 0:1]
        inv_floor = col_ref[hh, :, 1:2]
        w_add = row_ref[hh, 0:1, :]
        decay = row_ref[hh, 1:2, 0:1]
        q = q_ref[0, :, hh * dk:(hh + 1) * dk]
        kt = kt_ref[0, hh * dk:(hh + 1) * dk, :]
        vext_ref[hh, :, 0:dv] = v_ref[0, :, hh * dv:(hh + 1) * dv].astype(vext_ref.dtype)
        vext_ref[hh, :, dv:] = ones_cols
        vext = vext_ref[hh]

        s_qk = jnp.dot(q, kt, preferred_element_type=F32) * wi_ref[hh]
        intra = jnp.dot(s_qk.astype(BF16), vext, preferred_element_type=F32)
        ct = ct_ref[hh]
        inter = jnp.dot(q, ct.astype(BF16), preferred_element_type=F32)
        tot = intra + w_inter * inter
        den = tot[:, dv:dv + 1]
        h = tot[:, 0:dv] / jnp.maximum(jnp.abs(den), inv_floor)

        ktw = (kt.astype(F32) * w_add).astype(BF16)
        ct_ref[hh] = decay * ct + jnp.dot(ktw, vext, preferred_element_type=F32)

        cols = slice(hh * dv, (hh + 1) * dv)
        if final:
            hs = h + hprev_ref[0, :, cols]
            hn = hs * lax.rsqrt(jnp.mean(hs * hs, axis=-1, keepdims=True) + RMS_EPS)
            o_ref[0, :, cols] = (hn * hg_ref[:, cols]
                                 * jax.nn.sigmoid(og_ref[0, :, cols])).astype(o_ref.dtype)
        else:
            o_ref[0, :, cols] = h


def _mlstm_scan(bias, q, kt, z, v_blk, gates_t, reverse, hprev=None, o_blk=None, head_g=None):
    bsz, s, qkw = q.shape
    nheads = qkw // MLSTM_DK
    ln = min(MLSTM_CHUNK, s)
    nc = s // ln
    final = hprev is not None

    def cidx(c):
        return (nc - 1 - c) if reverse else c

    vw = nheads * MLSTM_DV
    wide = lambda blk: pl.BlockSpec((1, ln, vw), lambda b, c: (b, cidx(c), blk))
    in_specs = [pl.BlockSpec(memory_space=pltpu.SMEM),
                pl.BlockSpec((1, ln, qkw), lambda b, c: (b, cidx(c), 0)),
                pl.BlockSpec((1, qkw, ln), lambda b, c: (b, 0, cidx(c))),
                wide(v_blk),
                pl.BlockSpec((None, 4, nheads, 1, ln), lambda b, c: (b, 0, 0, 0, cidx(c)))]
    args = [bias, q, kt, z, gates_t]
    if final:
        in_specs += [wide(0), wide(o_blk), pl.BlockSpec((1, vw), lambda b, c: (0, 0))]
        args += [hprev, z, head_g]
    kern = functools.partial(_mlstm_kernel, chunk=ln, nheads=nheads, reverse=reverse, final=final)
    return pl.pallas_call(
        kern,
        grid=(bsz, nc),
        in_specs=in_specs,
        out_specs=wide(0),
        out_shape=jax.ShapeDtypeStruct((bsz, s, vw), BF16 if final else F32),
        scratch_shapes=[pltpu.VMEM((nheads, MLSTM_DK, MLSTM_DV + LANES), F32),
                        pltpu.VMEM((nheads, SUBLANES, LANES), F32),
                        pltpu.VMEM((nheads, ln, MLSTM_DV + LANES), BF16),
                        pltpu.VMEM((nheads, ln, ln), F32),
                        pltpu.VMEM((nheads, ln, LANES), F32),
                        pltpu.VMEM((nheads, SUBLANES, ln), F32)],
        compiler_params=_params(("parallel", "arbitrary")),
        name="mlstm_fwd" if final else "mlstm_bwd",
    )(*args)


ATTN_TILE = 1024
ATTN_HALO = ATTN_HALF * 16
ATTN_SUB = 128


def _band_bias(nq, nk, slope_d):
    qi = lax.broadcasted_iota(jnp.int32, (nq, nk), 0)
    kj = lax.broadcasted_iota(jnp.int32, (nq, nk), 1)
    rel = jnp.abs(kj - ATTN_HALF - qi)
    return jnp.where(rel <= ATTN_HALF, (-LOG2E * slope_d) * rel.astype(F32), MASK_VALUE)


def _band_scores(qs, ks, bias, key_lo, key_hi):
    s = lax.dot_general(qs.astype(BF16), ks.astype(BF16), (((1,), (1,)), ((), ())),
                        preferred_element_type=F32) + bias
    if key_lo is not None or key_hi is not None:
        kj = lax.broadcasted_iota(jnp.int32, s.shape, 1)
        inside = None
        if key_lo is not None:
            inside = kj >= key_lo
        if key_hi is not None:
            inside = (kj < key_hi) if inside is None else (inside & (kj < key_hi))
        s = jnp.where(inside, s, MASK_VALUE)
    m = jnp.max(s, axis=1, keepdims=True)
    return s, jnp.broadcast_to(m, (s.shape[0], LANES))


def _band_softmax_pv(s, m_b, vs):
    nq, nk = s.shape
    reps = [m_b] * (nk // LANES) + ([m_b[:, :nk % LANES]] if nk % LANES else [])
    m_wide = reps[0] if len(reps) == 1 else jnp.concatenate(reps, axis=1)
    p = jnp.exp2(s - m_wide).astype(BF16)
    v_ext = jnp.concatenate([vs.astype(BF16), jnp.ones((nk, LANES), BF16)], axis=1)
    r = jnp.dot(p, v_ext, preferred_element_type=F32)
    l_b = r[:, ATTN_HEAD_DIM:]
    return r[:, :ATTN_HEAD_DIM] / l_b, m_b * (1.0 / LOG2E) + jnp.log(l_b)


def _attn_kernel(slopes_ref, q_ref, kp_ref, kc_ref, kn_ref, vp_ref, vc_ref, vn_ref, g_ref, y_ref,
                 q4, k4, v4, acc1, lse1, acc2, lse2, acc3, lse3, ynat, sbuf, mbuf, *, tq):
    t = pl.program_id(1)
    half = ATTN_HALF
    dh = ATTN_HEAD_DIM
    halo = ATTN_HALO
    n4, h4 = tq // 4, halo // 4
    n16 = tq // 16
    sub = ATTN_SUB
    scale = LOG2E * ATTN_HEAD_DIM ** -0.5
    first_tile = t == 0
    last_tile = t == pl.num_programs(1) - 1
    slope = slopes_ref[pl.program_id(2)]

    for c in range(4):
        q4[c] = q_ref[0, pl.ds(c, n4, stride=4), :] * scale
        for dst, (p_ref, c_ref, n_ref) in ((k4, (kp_ref, kc_ref, kn_ref)),
                                           (v4, (vp_ref, vc_ref, vn_ref))):
            dst[c, 0:h4, :] = p_ref[0, pl.ds(c, h4, stride=4), :]
            dst[c, h4:h4 + n4, :] = c_ref[0, pl.ds(c, n4, stride=4), :]
            dst[c, h4 + n4:, :] = n_ref[0, pl.ds(c, h4, stride=4), :]

    def bounds(j, nblk, nsub):
        lo = jnp.where(first_tile, half, 0) if j == 0 else None
        hi = jnp.where(last_tile, nsub + half, nsub + 2 * half) if j == nblk - 1 else None
        return lo, hi

    def run_blocks(blocks, nsub, slope_d):
        nk = nsub + 2 * half
        bias = _band_bias(nsub, nk, slope_d)
        for i, (load_q, load_k, _, lo, hi, _) in enumerate(blocks):
            s, m_b = _band_scores(load_q(), load_k(), bias, lo, hi)
            sbuf[i, 0:nsub, 0:nk] = s
            mbuf[i, 0:nsub, :] = m_b
        for i, (_, _, load_v, _, _, store) in enumerate(blocks):
            store(*_band_softmax_pv(sbuf[i, 0:nsub, 0:nk], mbuf[i, 0:nsub, :], load_v()))

    def window(refs, r0, nk):
        p_ref, c_ref, n_ref = refs
        parts = []
        if r0 < 0:
            parts.append(p_ref[0, halo + r0:halo, :])
        parts.append(c_ref[0, max(r0, 0):min(r0 + nk, tq), :])
        if r0 + nk > tq:
            parts.append(n_ref[0, 0:r0 + nk - tq, :])
        return parts[0] if len(parts) == 1 else jnp.concatenate(parts, axis=0)

    def store_to(acc, lse, *idx):
        def store(o, lse_b):
            acc[idx] = o
            lse[idx] = lse_b
        return store

    krefs, vrefs = (kp_ref, kc_ref, kn_ref), (vp_ref, vc_ref, vn_ref)
    nk = sub + 2 * half
    nblk = tq // sub
    blocks = []
    for j in range(nblk):
        rows = slice(j * sub, (j + 1) * sub)
        blocks.append((lambda rows=rows: q_ref[0, rows, :] * scale,
                       lambda j=j: window(krefs, j * sub - half, nk),
                       lambda j=j: window(vrefs, j * sub - half, nk),
                       *bounds(j, nblk, sub), store_to(acc1, lse1, rows, slice(None))))
    run_blocks(blocks, sub, slope)

    nblk = n4 // sub
    blocks = []
    for c in range(4):
        for j in range(nblk):
            rows = slice(j * sub, (j + 1) * sub)
            krows = slice(h4 + j * sub - half, h4 + j * sub - half + nk)
            blocks.append((lambda c=c, rows=rows: q4[c, rows, :],
                           lambda c=c, krows=krows: k4[c, krows, :],
                           lambda c=c, krows=krows: v4[c, krows, :],
                           *bounds(j, nblk, sub), store_to(acc2, lse2, c, rows, slice(None))))
    run_blocks(blocks, sub, slope * 4)

    nsub = min(sub, n16)
    nk = nsub + 2 * half
    nblk = n16 // nsub
    blocks = []
    for c in range(4):
        for c16 in range(4):
            for j in range(nblk):
                rows_q = pl.ds(c16 + 4 * j * nsub, nsub, stride=4)
                rows_k = pl.ds(h4 + c16 + 4 * (j * nsub - half), nk, stride=4)
                blocks.append((lambda c=c, rows_q=rows_q: q4[c, rows_q, :],
                               lambda c=c, rows_k=rows_k: k4[c, rows_k, :],
                               lambda c=c, rows_k=rows_k: v4[c, rows_k, :],
                               *bounds(j, nblk, nsub),
                               store_to(acc3, lse3, c, rows_q, slice(None))))
    run_blocks(blocks, nsub, slope * 16)

    for c in range(4):
        rows = pl.ds(c, n4, stride=4)
        l1, l2, l3 = lse1[rows, :], lse2[c], lse3[c]
        mx = jnp.maximum(jnp.maximum(l1, l2), l3)
        e1, e2, e3 = jnp.exp(l1 - mx), jnp.exp(l2 - mx), jnp.exp(l3 - mx)
        o = (e1 * acc1[rows, :] + e2 * acc2[c] + e3 * acc3[c]) / (e1 + e2 + e3)
        on = o * lax.rsqrt(jnp.mean(o * o, axis=-1, keepdims=True) + RMS_EPS)
        ynat[rows, :] = on * g_ref[...]
    y_ref[0] = ynat[...].astype(y_ref.dtype)


def _dilated_attention(slopes, qkv, head_g):
    bsz, s, width3 = qkv.shape
    width = width3 // 3
    tq = min(ATTN_TILE, s)
    halo = ATTN_HALO
    assert tq % halo == 0 and s % tq == 0 and (tq // 16) % min(ATTN_SUB, tq // 16) == 0
    dh = ATTN_HEAD_DIM
    nh = width // dh
    nhb = tq // halo
    last_hb = s // halo - 1
    n4, h4 = tq // 4, halo // 4
    cur = lambda o: pl.BlockSpec((1, tq, dh), lambda b, t, c, sl: (b, t, o + c))
    prev = lambda o: pl.BlockSpec((1, halo, dh),
                                  lambda b, t, c, sl: (b, jnp.maximum(t * nhb - 1, 0), o + c))
    nxt = lambda o: pl.BlockSpec((1, halo, dh),
                                 lambda b, t, c, sl: (b, jnp.minimum((t + 1) * nhb, last_hb), o + c))
    kern = functools.partial(_attn_kernel, tq=tq)
    nstage = max(tq // ATTN_SUB, 16 * ((tq // 16) // min(ATTN_SUB, tq // 16)))
    win = pltpu.VMEM((4, n4 + 2 * h4, dh), F32)
    cls = pltpu.VMEM((4, n4, dh), F32)
    nat = pltpu.VMEM((tq, dh), F32)
    return pl.pallas_call(
        kern,
        grid_spec=pltpu.PrefetchScalarGridSpec(
            num_scalar_prefetch=1,
            grid=(bsz, s // tq, width // dh),
            in_specs=[cur(0), prev(nh), cur(nh), nxt(nh), prev(2 * nh), cur(2 * nh), nxt(2 * nh),
                      pl.BlockSpec((1, dh), lambda b, t, c, sl: (0, c))],
            out_specs=pl.BlockSpec((1, tq, dh), lambda b, t, c, sl: (b, t, c)),
            scratch_shapes=[cls, win, win, nat, nat, cls, cls, cls, cls, nat,
                            pltpu.VMEM((nstage, ATTN_SUB, ATTN_SUB + 2 * ATTN_HALF), F32),
                            pltpu.VMEM((nstage, ATTN_SUB, LANES), F32)]),
        out_shape=jax.ShapeDtypeStruct((bsz, s, width), BF16),
        compiler_params=_params(("parallel", "parallel", "parallel")),
        name="dilated_attn",
    )(slopes, qkv, qkv, qkv, qkv, qkv, qkv, qkv, head_g.reshape(1, width))


def _start_row_gather(idx_ref, src_hbm, dst, sem, nrows):
    def body(r, carry):
        pltpu.make_async_copy(src_hbm.at[pl.ds(idx_ref[0, 0, r], 1), :],
                              dst.at[pl.ds(r, 1), :], sem).start()
        return carry

    lax.fori_loop(0, nrows, body, 0, unroll=8)


def _wait_row_gather(src_hbm, dst, sem, nrows):
    pltpu.make_async_copy(src_hbm.at[pl.ds(0, nrows), :], dst, sem).wait()


def _round_weights(src_ref, dst_ref):
    rows_total = dst_ref.shape[0]
    step = min(WEIGHT_CAST_ROWS, rows_total)

    def body(c, carry):
        rows = pl.ds(pl.multiple_of(c * step, step), step)
        dst_ref[rows, :] = src_ref[rows, :].astype(dst_ref.dtype)
        return carry

    lax.fori_loop(0, rows_total // step, body, 0)


def _expert_changed(be_ref, i):
    return (i == 0) | (be_ref[i] != be_ref[jnp.maximum(i - 1, 0)])


def _expert_up_kernel(be_ref, nb_ref, idx_ref, idx_next_ref, h_hbm, wg_ref, wu_ref, o_ref,
                      xbuf, sem, wgb_ref, wub_ref):
    i = pl.program_id(0)
    n_used = nb_ref[0]
    slot = i % 2
    blk = xbuf.shape[1]

    @pl.when(i == 0)
    def _():
        _start_row_gather(idx_ref, h_hbm, xbuf.at[0], sem.at[0], blk)

    @pl.when(i + 1 < n_used)
    def _():
        _start_row_gather(idx_next_ref, h_hbm, xbuf.at[1 - slot], sem.at[1 - slot], blk)

    @pl.when((i < n_used) & _expert_changed(be_ref, i))
    def _():
        _round_weights(wg_ref, wgb_ref)
        _round_weights(wu_ref, wub_ref)

    @pl.when(i < n_used)
    def _():
        _wait_row_gather(h_hbm, xbuf.at[slot], sem.at[slot], blk)
        x = xbuf[slot].astype(BF16)
        g = jnp.dot(x, wgb_ref[...], preferred_element_type=F32)
        u = jnp.dot(x, wub_ref[...], preferred_element_type=F32)
        o_ref[...] = (g * jax.nn.sigmoid(g) * u).astype(o_ref.dtype)

    @pl.when(i >= n_used)
    def _():
        o_ref[...] = jnp.zeros_like(o_ref)


HI16 = 0xFFFF0000


def _pack_bf16_pairs(lo, hi):
    bits = lambda v: lax.bitcast_convert_type(v.astype(BF16).astype(F32), jnp.uint32)
    return (bits(hi) & jnp.uint32(HI16)) | (bits(lo) >> 16)


def _unpack_bf16_pairs(p):
    lo = lax.bitcast_convert_type(p << 16, F32)
    hi = lax.bitcast_convert_type(p & jnp.uint32(HI16), F32)
    return lo, hi


def _expert_down_kernel(be_ref, nb_ref, h_ref, wd_ref, o_ref, wdb_ref):
    i = pl.program_id(0)
    half = o_ref.shape[1]

    @pl.when((i < nb_ref[0]) & _expert_changed(be_ref, i))
    def _():
        _round_weights(wd_ref, wdb_ref)

    @pl.when(i < nb_ref[0])
    def _():
        y = jnp.dot(h_ref[...], wdb_ref[...], preferred_element_type=F32)
        o_ref[...] = _pack_bf16_pairs(y[:, :half], y[:, half:])

    @pl.when(i >= nb_ref[0])
    def _():
        o_ref[...] = jnp.zeros_like(o_ref)


def _expert_ffn(h, slot_tok, block_e, n_used, w_gate, w_up, w_down, layer):
    d = h.shape[1]
    p = slot_tok.shape[0]
    ff = w_gate.shape[3]
    blk = EXPERT_BLOCK
    nblk = p // blk
    idx = slot_tok.reshape(nblk, 1, blk)
    h1 = pl.pallas_call(
        _expert_up_kernel,
        grid_spec=pltpu.PrefetchScalarGridSpec(
            num_scalar_prefetch=2,
            grid=(nblk,),
            in_specs=[pl.BlockSpec((1, 1, blk), lambda i, be, nb: (i, 0, 0),
                                   memory_space=pltpu.SMEM),
                      pl.BlockSpec((1, 1, blk), lambda i, be, nb: (jnp.minimum(i + 1, nblk - 1), 0, 0),
                                   memory_space=pltpu.SMEM),
                      pl.BlockSpec(memory_space=pl.ANY),
                      pl.BlockSpec((None, None, d, ff), lambda i, be, nb: (layer, be[i], 0, 0),
                                   pipeline_mode=pl.Buffered(1)),
                      pl.BlockSpec((None, None, d, ff), lambda i, be, nb: (layer, be[i], 0, 0),
                                   pipeline_mode=pl.Buffered(1))],
            out_specs=pl.BlockSpec((blk, ff), lambda i, be, nb: (i, 0)),
            scratch_shapes=[pltpu.VMEM((2, blk, d), F32), pltpu.SemaphoreType.DMA((2,)),
                            pltpu.VMEM((d, ff), BF16), pltpu.VMEM((d, ff), BF16)]),
        out_shape=jax.ShapeDtypeStruct((p, ff), BF16),
        compiler_params=_params(("arbitrary",)),
        name="expert_up",
    )(block_e, n_used, idx, idx, h, w_gate, w_up)
    return pl.pallas_call(
        _expert_down_kernel,
        grid_spec=pltpu.PrefetchScalarGridSpec(
            num_scalar_prefetch=2,
            grid=(nblk,),
            in_specs=[pl.BlockSpec((blk, ff), lambda i, be, nb: (i, 0)),
                      pl.BlockSpec((None, None, ff, d), lambda i, be, nb: (layer, be[i], 0, 0))],
            out_specs=pl.BlockSpec((blk, d // 2), lambda i, be, nb: (i, 0)),
            scratch_shapes=[pltpu.VMEM((ff, d), BF16)]),
        out_shape=jax.ShapeDtypeStruct((p, d // 2), jnp.uint32),
        compiler_params=_params(("arbitrary",)),
        name="expert_down",
    )(block_e, n_used, h1, w_down)


def _combine_kernel(p0_ref, p1_ref, p0n_ref, p1n_ref, x_ref, w0_ref, w1_ref, g_ref, y_hbm,
                    *rest, emit_x):
    if emit_x:
        xo_ref, ho_ref, buf, sem = rest
    else:
        ho_ref, buf, sem = rest
    i = pl.program_id(0)
    nt = pl.num_programs(0)
    slot = i % 2
    tm = buf.shape[2]

    @pl.when(i == 0)
    def _():
        _start_row_gather(p0_ref, y_hbm, buf.at[0, 0], sem.at[0], tm)
        _start_row_gather(p1_ref, y_hbm, buf.at[0, 1], sem.at[0], tm)

    @pl.when(i + 1 < nt)
    def _():
        _start_row_gather(p0n_ref, y_hbm, buf.at[1 - slot, 0], sem.at[1 - slot], tm)
        _start_row_gather(p1n_ref, y_hbm, buf.at[1 - slot, 1], sem.at[1 - slot], tm)

    _wait_row_gather(y_hbm, buf.at[slot, 0], sem.at[slot], tm)
    _wait_row_gather(y_hbm, buf.at[slot, 1], sem.at[slot], tm)
    half = buf.shape[3]
    lo0, hi0 = _unpack_bf16_pairs(buf[slot, 0])
    lo1, hi1 = _unpack_bf16_pairs(buf[slot, 1])
    w0, w1 = w0_ref[...], w1_ref[...]
    x_lo = x_ref[:, :half] + w0 * lo0 + w1 * lo1
    x_hi = x_ref[:, half:] + w0 * hi0 + w1 * hi1
    if emit_x:
        xo_ref[:, :half] = x_lo
        xo_ref[:, half:] = x_hi
    ssq = (jnp.sum(x_lo * x_lo, axis=-1, keepdims=True)
           + jnp.sum(x_hi * x_hi, axis=-1, keepdims=True))
    inv = lax.rsqrt(ssq * (1.0 / (2 * half)) + RMS_EPS)
    ho_ref[:, :half] = (x_lo * inv * g_ref[:, :half]).astype(ho_ref.dtype)
    ho_ref[:, half:] = (x_hi * inv * g_ref[:, half:]).astype(ho_ref.dtype)


def _combine_norm(x2d, y, pos, wts, g, emit_x, tm=256):
    t, d = x2d.shape
    nt = t // tm
    pidx = pos.reshape(EXPERT_TOPK, nt, 1, tm)
    wcol = wts.reshape(EXPERT_TOPK, t, 1)
    cur = lambda i: (i, 0, 0)
    nxt = lambda i: (jnp.minimum(i + 1, nt - 1), 0, 0)
    ispec = lambda f: pl.BlockSpec((1, 1, tm), f, memory_space=pltpu.SMEM)
    row = pl.BlockSpec((tm, d), lambda i: (i, 0))
    wspec = pl.BlockSpec((tm, 1), lambda i: (i, 0))
    if emit_x:
        out_specs = [row, row]
        out_shape = [jax.ShapeDtypeStruct((t, d), F32), jax.ShapeDtypeStruct((t, d), BF16)]
    else:
        out_specs = row
        out_shape = jax.ShapeDtypeStruct((t, d), F32)
    return pl.pallas_call(
        functools.partial(_combine_kernel, emit_x=emit_x),
        grid=(nt,),
        in_specs=[ispec(cur), ispec(cur), ispec(nxt), ispec(nxt), row, wspec, wspec,
                  pl.BlockSpec((1, d), lambda i: (0, 0)),
                  pl.BlockSpec(memory_space=pl.ANY)],
        out_specs=out_specs,
        out_shape=out_shape,
        scratch_shapes=[pltpu.VMEM((2, EXPERT_TOPK, tm, d // 2), jnp.uint32),
                        pltpu.SemaphoreType.DMA((2,))],
        compiler_params=_params(("arbitrary",)),
        name="moe_combine_norm",
    )(pidx[0], pidx[1], pidx[0], pidx[1], x2d, wcol[0], wcol[1], g.reshape(1, d), y)


def _route(eid, n_experts):
    t = eid.shape[1]
    a = EXPERT_TOPK * t
    flat_e = eid.reshape(a)
    onehot = (flat_e[:, None] == jnp.arange(n_experts, dtype=jnp.int32)[None, :]).astype(jnp.int32)
    csum = jnp.cumsum(onehot, axis=0)
    counts = csum[-1]
    blocks = (counts + EXPERT_BLOCK - 1) // EXPERT_BLOCK
    blk_end = jnp.cumsum(blocks)
    pad_start = (blk_end - blocks) * EXPERT_BLOCK
    dest = jnp.sum(onehot * (csum - 1 + pad_start[None, :]), axis=1)
    n_blocks = -(-a // EXPERT_BLOCK) + n_experts
    p = n_blocks * EXPERT_BLOCK
    tok = jnp.arange(a, dtype=jnp.int32) % t
    slot_tok = jnp.zeros((p,), jnp.int32).at[dest].set(tok)
    block_e = jnp.sum((blk_end[None, :] <= jnp.arange(n_blocks, dtype=jnp.int32)[:, None])
                      .astype(jnp.int32), axis=1)
    block_e = jnp.minimum(block_e, n_experts - 1)
    n_used = blk_end[-1:].astype(jnp.int32)
    return slot_tok, block_e, n_used, dest.reshape(EXPERT_TOPK, t)


def kernel(x, norm1_g, w_in, b_gate, conv_w, conv_b, head_norm_g, w_out, norm2_g, w_router_group,
           b_router_group, w_router_expert, b_router_expert, w_gate, w_up, w_down, final_norm_g):
    depth = norm1_g.shape[0]
    bsz, s, d = x.shape
    t = bsz * s
    mw = d // 2
    nmh = mw // MLSTM_DV
    qkw = nmh * MLSTM_DK
    aw = d - mw
    nah = aw // ATTN_HEAD_DIM
    ngate = 4 * nmh
    n_groups = w_router_group.shape[2]
    n_experts = w_router_expert.shape[2]

    c_v = 2 * qkw
    c_o = c_v + mw
    c_g = c_o + mw
    c_a = c_g + ngate
    assert ngate <= LANES and c_a + 3 * aw == w_in.shape[2]
    w_in_t = jnp.swapaxes(w_in, 1, 2)
    npad = ROUTER_ROWS - n_groups - n_experts
    w_router_t = jnp.pad(jnp.concatenate([w_router_group, w_router_expert], axis=2)
                         .transpose(0, 2, 1), ((0, 0), (0, npad), (0, 0)))
    b_router_t = jnp.pad(jnp.concatenate([b_router_group, b_router_expert], axis=1),
                         ((0, 0), (0, npad))).reshape(depth, ROUTER_ROWS, 1)
    slopes = jnp.exp2(-8.0 * jnp.arange(1, nah + 1, dtype=F32) / nah)

    x2d = x.reshape(t, d)
    h = _rmsnorm(x2d, norm1_g[0], BF16)
    for l in range(depth):
        head_g = head_norm_g[l]
        z_m = _matmul(h, w_in_t, l, F32, row0=0, n=c_g).reshape(bsz, s, c_g)
        gates = _matmul(h, w_in_t, l, F32, row0=c_g, n=LANES)[:, :ngate]
        z_a = _matmul(h, w_in_t, l, F32, row0=c_a, n=3 * aw).reshape(bsz, s, 3 * aw)

        q_m = _short_conv_silu(z_m, conv_w[l], conv_b[l], 0, qkw, MLSTM_DK ** -0.5,
                               transpose=False)
        kt_m = _short_conv_silu(z_m, conv_w[l], conv_b[l], qkw, qkw, 1.0, transpose=True)
        gates_t = gates.reshape(bsz, s, 4, nmh).transpose(0, 2, 3, 1).reshape(bsz, 4, nmh, 1, s)
        h_bwd = _mlstm_scan(b_gate[l], q_m, kt_m, z_m, 1, gates_t, reverse=True)
        y_m = _mlstm_scan(b_gate[l], q_m, kt_m, z_m, 1, gates_t, reverse=False, hprev=h_bwd,
                          o_blk=2, head_g=head_g[:mw].reshape(1, mw))

        y_a = _dilated_attention(slopes, z_a, head_g[mw:])

        x2d = _out_proj(y_m.reshape(t, mw), y_a.reshape(t, aw), w_out, l, x2d)

        h2, eid, wts = _rmsnorm_router(x2d, norm2_g[l], w_router_t[l], b_router_t[l],
                                       n_groups, n_experts // n_groups)
        slot_tok, block_e, n_used, pos = _route(eid[:EXPERT_TOPK], n_experts)
        yb = _expert_ffn(h2, slot_tok, block_e, n_used, w_gate, w_up, w_down, l)
        if l + 1 < depth:
            x2d, h = _combine_norm(x2d, yb, pos, wts[:EXPERT_TOPK], norm1_g[l + 1], emit_x=True)
        else:
            out = _combine_norm(x2d, yb, pos, wts[:EXPERT_TOPK], final_norm_g, emit_x=False)
    return out.reshape(bsz, s, d)
```

```python
import functools

import jax
import jax.numpy as jnp
from jax import lax
from jax.experimental import pallas as pl
from jax.experimental.pallas import tpu as pltpu

MLSTM_DV = 512
MLSTM_DK = 256
ATTN_HEAD_DIM = 128
CONV_W = 5
DILATED_PATTERNS = ((128, 1), (512, 4), (2048, 16))
ATTN_HALF = 64
EXPERT_TOPK = 2
RMS_EPS = 1e-6
MASK_VALUE = -1e30
LOG2E = 1.4426950408889634

LANES = 128
SUBLANES = 8
VMEM_LIMIT_BYTES = 56 * 1024 * 1024
PROJ_VMEM_LIMIT_BYTES = 60 * 1024 * 1024

MLSTM_CHUNK = 256
EXPERT_BLOCK = 256
ROUTER_ROWS = 32

BF16 = jnp.bfloat16
F32 = jnp.float32


def _params(semantics):
    return pltpu.CompilerParams(dimension_semantics=semantics, vmem_limit_bytes=VMEM_LIMIT_BYTES)


def _rmsnorm_kernel(x_ref, g_ref, o_ref):
    x = x_ref[...]
    ms = jnp.mean(x * x, axis=-1, keepdims=True)
    o_ref[...] = (x * lax.rsqrt(ms + RMS_EPS) * g_ref[...]).astype(o_ref.dtype)


def _rmsnorm(x2d, g, out_dtype, tm=256):
    t, d = x2d.shape
    return pl.pallas_call(
        _rmsnorm_kernel,
        grid=(t // tm,),
        in_specs=[pl.BlockSpec((tm, d), lambda i: (i, 0)),
                  pl.BlockSpec((1, d), lambda i: (0, 0))],
        out_specs=pl.BlockSpec((tm, d), lambda i: (i, 0)),
        out_shape=jax.ShapeDtypeStruct((t, d), out_dtype),
        compiler_params=_params(("parallel",)),
        name="rmsnorm",
    )(x2d, g.reshape(1, d))


def _first_argmax(vals):
    best = vals[0]
    idx = jnp.zeros(best.shape, jnp.int32)
    for i in range(1, len(vals)):
        gt = vals[i] > best
        best = jnp.where(gt, vals[i], best)
        idx = jnp.where(gt, i, idx)
    return best, idx


def _softmax_list(vals):
    mx = functools.reduce(jnp.maximum, vals)
    es = [jnp.exp(v - mx) for v in vals]
    tot = functools.reduce(lambda a, b: a + b, es)
    return [e / tot for e in es]


def _rmsnorm_router_kernel(x_ref, g_ref, wrt_ref, brt_ref, h_ref, eid_ref, wt_ref,
                           *, n_groups, per_group):
    x = x_ref[...]
    ms = jnp.mean(x * x, axis=-1, keepdims=True)
    h = x * lax.rsqrt(ms + RMS_EPS) * g_ref[...]
    h_ref[...] = h.astype(h_ref.dtype)
    def split(v):
        hi = v.astype(BF16)
        return hi, (v - hi.astype(F32)).astype(BF16)

    def dot_nt(a, b):
        return lax.dot_general(a, b, (((1,), (1,)), ((), ())), preferred_element_type=F32)

    w_hi, w_lo = split(wrt_ref[...])
    h_hi, h_lo = split(h)
    lt = dot_nt(w_hi, h_hi) + dot_nt(w_lo, h_hi) + dot_nt(w_hi, h_lo) + brt_ref[...]
    g_prob = _softmax_list([lt[i:i + 1, :] for i in range(n_groups)])
    g_top_p, g_top = _first_argmax(g_prob)
    e_sel = []
    for j in range(per_group):
        row = lt[n_groups + j:n_groups + j + 1, :]
        for grp in range(1, n_groups):
            r = n_groups + grp * per_group + j
            row = jnp.where(g_top == grp, lt[r:r + 1, :], row)
        e_sel.append(row)
    e_prob = _softmax_list(e_sel)
    p1, i1 = _first_argmax(e_prob)
    p2, i2 = _first_argmax([jnp.where(i1 == j, -1.0, e_prob[j]) for j in range(per_group)])
    denom = p1 + p2
    w1 = g_top_p * p1 / denom
    w2 = g_top_p * p2 / denom
    e1 = g_top * per_group + i1
    e2 = g_top * per_group + i2
    rows = lax.broadcasted_iota(jnp.int32, eid_ref.shape, 0)
    eid_ref[...] = jnp.where(rows == 0, e1, jnp.where(rows == 1, e2, 0))
    wt_ref[...] = jnp.where(rows == 0, w1, jnp.where(rows == 1, w2, 0.0))


def _rmsnorm_router(x2d, g, w_router_t, b_router_t, n_groups, per_group, tm=256):
    t, d = x2d.shape
    kern = functools.partial(_rmsnorm_router_kernel, n_groups=n_groups, per_group=per_group)
    return pl.pallas_call(
        kern,
        grid=(t // tm,),
        in_specs=[pl.BlockSpec((tm, d), lambda i: (i, 0)),
                  pl.BlockSpec((1, d), lambda i: (0, 0)),
                  pl.BlockSpec((ROUTER_ROWS, d), lambda i: (0, 0)),
                  pl.BlockSpec((ROUTER_ROWS, 1), lambda i: (0, 0))],
        out_specs=[pl.BlockSpec((tm, d), lambda i: (i, 0)),
                   pl.BlockSpec((SUBLANES, tm), lambda i: (0, i)),
                   pl.BlockSpec((SUBLANES, tm), lambda i: (0, i))],
        out_shape=[jax.ShapeDtypeStruct((t, d), F32),
                   jax.ShapeDtypeStruct((SUBLANES, t), jnp.int32),
                   jax.ShapeDtypeStruct((SUBLANES, t), F32)],
        compiler_params=_params(("parallel",)),
        name="rmsnorm_router",
    )(x2d, g.reshape(1, d), w_router_t, b_router_t)


WEIGHT_CAST_ROWS = 256


def _matmul_kernel(a_ref, bt_ref, o_ref, bb_ref):
    @pl.when(pl.program_id(1) == 0)
    def _():
        step = min(WEIGHT_CAST_ROWS, bb_ref.shape[0])

        def body(c, carry):
            rows = pl.ds(pl.multiple_of(c * step, step), step)
            bb_ref[rows, :] = bt_ref[0, rows, :].astype(BF16)
            return carry

        lax.fori_loop(0, bb_ref.shape[0] // step, body, 0)

    o_ref[...] = lax.dot_general(a_ref[...], bb_ref[...], (((1,), (1,)), ((), ())),
                                 preferred_element_type=F32).astype(o_ref.dtype)


def _matmul(a, bt, layer, out_dtype, row0, n, tm=512, tn=1024):
    m, k = a.shape
    tn = min(tn, n)
    tm = min(tm, m)
    assert row0 % SUBLANES == 0 and n % tn == 0 and m % tm == 0
    return pl.pallas_call(
        _matmul_kernel,
        grid=(n // tn, m // tm),
        in_specs=[pl.BlockSpec((tm, k), lambda j, i: (i, 0)),
                  pl.BlockSpec((pl.Element(1), pl.Element(tn), pl.Element(k)),
                               lambda j, i: (layer, pl.multiple_of(row0 + j * tn, SUBLANES), 0))],
        out_specs=pl.BlockSpec((tm, tn), lambda j, i: (i, j)),
        out_shape=jax.ShapeDtypeStruct((m, n), out_dtype),
        scratch_shapes=[pltpu.VMEM((tn, k), BF16)],
        compiler_params=pltpu.CompilerParams(dimension_semantics=("parallel", "arbitrary"),
                                             vmem_limit_bytes=PROJ_VMEM_LIMIT_BYTES),
        name="proj_in",
    )(a, bt)


def _out_proj_kernel(a1_ref, a2_ref, b1_ref, b2_ref, r_ref, o_ref, bb1_ref, bb2_ref):
    @pl.when(pl.program_id(1) == 0)
    def _():
        bb1_ref[...] = b1_ref[...].astype(BF16)
        bb2_ref[...] = b2_ref[...].astype(BF16)

    acc = jnp.dot(a1_ref[...], bb1_ref[...], preferred_element_type=F32)
    acc = acc + jnp.dot(a2_ref[...], bb2_ref[...], preferred_element_type=F32)
    o_ref[...] = r_ref[...] + acc


def _out_proj(a1, a2, w, layer, res, tm=512, tn=1024):
    m, k1 = a1.shape
    k2 = a2.shape[1]
    n = w.shape[2]
    tn = min(tn, n)
    tm = min(tm, m)
    assert k1 == k2
    return pl.pallas_call(
        _out_proj_kernel,
        grid=(n // tn, m // tm),
        in_specs=[pl.BlockSpec((tm, k1), lambda j, i: (i, 0)),
                  pl.BlockSpec((tm, k2), lambda j, i: (i, 0)),
                  pl.BlockSpec((None, k1, tn), lambda j, i: (layer, 0, j)),
                  pl.BlockSpec((None, k2, tn), lambda j, i: (layer, 1, j)),
                  pl.BlockSpec((tm, tn), lambda j, i: (i, j))],
        out_specs=pl.BlockSpec((tm, tn), lambda j, i: (i, j)),
        out_shape=jax.ShapeDtypeStruct((m, n), F32),
        scratch_shapes=[pltpu.VMEM((k1, tn), BF16), pltpu.VMEM((k2, tn), BF16)],
        compiler_params=pltpu.CompilerParams(dimension_semantics=("parallel", "arbitrary"),
                                             vmem_limit_bytes=PROJ_VMEM_LIMIT_BYTES),
        name="proj_out",
    )(a1, a2, w, w, res)


def _conv_kernel(up_ref, uc_ref, un_ref, w_ref, b_ref, o_ref, buf_ref, *, ts, scale, transpose):
    t = pl.program_id(2)
    nt = pl.num_programs(2)
    pad = SUBLANES
    buf_ref[0:pad, :] = jnp.where(t > 0, up_ref[0], 0.0)
    buf_ref[pad:pad + ts, :] = uc_ref[0]
    buf_ref[pad + ts:2 * pad + ts, :] = jnp.where(t < nt - 1, un_ref[0], 0.0)
    acc = jnp.zeros((ts, buf_ref.shape[1]), F32) + b_ref[...]
    for j in range(CONV_W):
        acc = acc + buf_ref[pl.ds(pad - CONV_W // 2 + j, ts), :] * w_ref[j:j + 1, :]
    y = acc * jax.nn.sigmoid(acc) * scale
    if transpose:
        o_ref[0] = y.T.astype(o_ref.dtype)
    else:
        o_ref[0] = y.astype(o_ref.dtype)


def _short_conv_silu(u, conv_w, conv_b, col0, width, scale, transpose, ts=512):
    bsz, s, _ = u.shape
    tc = MLSTM_DK
    ts = min(ts, s)
    cb0 = col0 // tc
    nrb = ts // SUBLANES
    last_rb = s // SUBLANES - 1
    kern = functools.partial(_conv_kernel, ts=ts, scale=scale, transpose=transpose)
    if transpose:
        out_spec = pl.BlockSpec((1, tc, ts), lambda b, c, t: (b, c, t))
        out_shape = jax.ShapeDtypeStruct((bsz, width, s), BF16)
    else:
        out_spec = pl.BlockSpec((1, ts, tc), lambda b, c, t: (b, t, c))
        out_shape = jax.ShapeDtypeStruct((bsz, s, width), BF16)
    return pl.pallas_call(
        kern,
        grid=(bsz, width // tc, s // ts),
        in_specs=[pl.BlockSpec((1, SUBLANES, tc),
                               lambda b, c, t: (b, jnp.maximum(t * nrb - 1, 0), cb0 + c)),
                  pl.BlockSpec((1, ts, tc), lambda b, c, t: (b, t, cb0 + c)),
                  pl.BlockSpec((1, SUBLANES, tc),
                               lambda b, c, t: (b, jnp.minimum((t + 1) * nrb, last_rb), cb0 + c)),
                  pl.BlockSpec((CONV_W, tc), lambda b, c, t: (0, cb0 + c)),
                  pl.BlockSpec((1, tc), lambda b, c, t: (0, cb0 + c))],
        out_specs=out_spec,
        out_shape=out_shape,
        scratch_shapes=[pltpu.VMEM((ts + 2 * SUBLANES, tc), F32)],
        compiler_params=_params(("parallel", "parallel", "parallel")),
        name="conv_silu_t" if transpose else "conv_silu",
    )(u, u, u, conv_w, conv_b.reshape(1, -1))


def _log_sigmoid(x):
    return jnp.minimum(x, 0.0) - jnp.log1p(jnp.exp(-jnp.abs(x)))


def _mlstm_kernel(bias_ref, q_ref, kt_ref, v_ref, g_ref, *rest, chunk, nheads, reverse, final):
    if final:
        hprev_ref, og_ref, hg_ref, o_ref, ct_ref, m_ref, vext_ref, wi_ref, col_ref, row_ref = rest
    else:
        o_ref, ct_ref, m_ref, vext_ref, wi_ref, col_ref, row_ref = rest
    c_idx = pl.program_id(1)
    dv, dk = MLSTM_DV, MLSTM_DK
    ln = chunk

    @pl.when(c_idx == 0)
    def _():
        ct_ref[...] = jnp.zeros_like(ct_ref)
        m_ref[...] = jnp.zeros_like(m_ref)

    gi, gf = (2, 3) if reverse else (0, 1)
    row = lax.broadcasted_iota(jnp.int32, (ln, ln), 0)
    col = lax.broadcasted_iota(jnp.int32, (ln, ln), 1)
    cum_mask = ((row >= col) if reverse else (row <= col)).astype(F32)
    tri = (col >= row) if reverse else (col <= row)
    diag = row == col
    ones_cols = jnp.ones((ln, LANES), vext_ref.dtype)

    i_rows = [g_ref[gi, hh] + bias_ref[gi * nheads + hh] for hh in range(nheads)]
    f_rows = [_log_sigmoid(g_ref[gf, hh] + bias_ref[gf * nheads + hh]) for hh in range(nheads)]
    f_all = jnp.concatenate(f_rows + [jnp.zeros((SUBLANES - nheads, ln), F32)], axis=0)
    b_all = jnp.dot(f_all, cum_mask, precision=lax.Precision.HIGHEST, preferred_element_type=F32)
    for hh in range(nheads):
        i_row = i_rows[hh]
        b_row = b_all[hh:hh + 1, :]
        total = b_row[:, 0:1] if reverse else b_row[:, ln - 1:ln]
        b_col = jnp.sum(jnp.where(diag, b_row, 0.0), axis=1, keepdims=True)
        m0 = m_ref[hh, 0:1, 0:1]
        d_log = jnp.where(tri, b_col - b_row + i_row, MASK_VALUE)
        a_col = b_col + m0
        m_t = jnp.maximum(a_col, jnp.max(d_log, axis=1, keepdims=True))
        wi_ref[hh] = jnp.exp(d_log - m_t)
        col_ref[hh, :, 0:1] = jnp.exp(a_col - m_t)
        col_ref[hh, :, 1:2] = jnp.exp(-m_t)
        g_row = total - b_row + i_row
        m_new = jnp.maximum(total + m0, jnp.max(g_row, axis=1, keepdims=True))
        row_ref[hh, 0:1, :] = jnp.exp(g_row - m_new)
        row_ref[hh, 1:2, :] = jnp.broadcast_to(jnp.exp(total + m0 - m_new), (1, ln))
        m_ref[hh] = jnp.broadcast_to(m_new, m_ref.shape[1:])

    for hh in range(nheads):
        w_inter = col_ref[hh, :, 0:1]
        inv_floor = col_ref[hh, :, 1:2]
        w_add = row_ref[hh, 0:1, :]
        decay = row_ref[hh, 1:2, 0:1]
        q = q_ref[0, :, hh * dk:(hh + 1) * dk]
        kt = kt_ref[0, hh * dk:(hh + 1) * dk, :]
        vext_ref[hh, :, 0:dv] = v_ref[0, :, hh * dv:(hh + 1) * dv].astype(vext_ref.dtype)
        vext_ref[hh, :, dv:] = ones_cols
        vext = vext_ref[hh]

        s_qk = jnp.dot(q, kt, preferred_element_type=F32) * wi_ref[hh]
        intra = jnp.dot(s_qk.astype(BF16), vext, preferred_element_type=F32)
        ct = ct_ref[hh]
        inter = jnp.dot(q, ct.astype(BF16), preferred_element_type=F32)
        tot = intra + w_inter * inter
        den = tot[:, dv:dv + 1]
        h = tot[:, 0:dv] / jnp.maximum(jnp.abs(den), inv_floor)

        ktw = (kt.astype(F32) * w_add).astype(BF16)
        ct_ref[hh] = decay * ct + jnp.dot(ktw, vext, preferred_element_type=F32)

        cols = slice(hh * dv, (hh + 1) * dv)
        if final:
            hs = h + hprev_ref[0, :, cols]
            hn = hs * lax.rsqrt(jnp.mean(hs * hs, axis=-1, keepdims=True) + RMS_EPS)
            o_ref[0, :, cols] = (hn * hg_ref[:, cols]
                                 * jax.nn.sigmoid(og_ref[0, :, cols])).astype(o_ref.dtype)
        else:
            o_ref[0, :, cols] = h


def _mlstm_scan(bias, q, kt, z, v_blk, gates_t, reverse, hprev=None, o_blk=None, head_g=None):
    bsz, s, qkw = q.shape
    nheads = qkw // MLSTM_DK
    ln = min(MLSTM_CHUNK, s)
    nc = s // ln
    final = hprev is not None

    def cidx(c):
        return (nc - 1 - c) if reverse else c

    vw = nheads * MLSTM_DV
    wide = lambda blk: pl.BlockSpec((1, ln, vw), lambda b, c: (b, cidx(c), blk))
    in_specs = [pl.BlockSpec(memory_space=pltpu.SMEM),
                pl.BlockSpec((1, ln, qkw), lambda b, c: (b, cidx(c), 0)),
                pl.BlockSpec((1, qkw, ln), lambda b, c: (b, 0, cidx(c))),
                wide(v_blk),
                pl.BlockSpec((None, 4, nheads, 1, ln), lambda b, c: (b, 0, 0, 0, cidx(c)))]
    args = [bias, q, kt, z, gates_t]
    if final:
        in_specs += [wide(0), wide(o_blk), pl.BlockSpec((1, vw), lambda b, c: (0, 0))]
        args += [hprev, z, head_g]
    kern = functools.partial(_mlstm_kernel, chunk=ln, nheads=nheads, reverse=reverse, final=final)
    return pl.pallas_call(
        kern,
        grid=(bsz, nc),
        in_specs=in_specs,
        out_specs=wide(0),
        out_shape=jax.ShapeDtypeStruct((bsz, s, vw), BF16 if final else F32),
        scratch_shapes=[pltpu.VMEM((nheads, MLSTM_DK, MLSTM_DV + LANES), F32),
                        pltpu.VMEM((nheads, SUBLANES, LANES), F32),
                        pltpu.VMEM((nheads, ln, MLSTM_DV + LANES), BF16),
                        pltpu.VMEM((nheads, ln, ln), F32),
                        pltpu.VMEM((nheads, ln, LANES), F32),
                        pltpu.VMEM((nheads, SUBLANES, ln), F32)],
        compiler_params=_params(("parallel", "arbitrary")),
        name="mlstm_fwd" if final else "mlstm_bwd",
    )(*args)


ATTN_TILE = 2048
ATTN_HALO = ATTN_HALF * 16
ATTN_SUB = 128


def _band_bias(nq, nk, slope_d):
    qi = lax.broadcasted_iota(jnp.int32, (nq, nk), 0)
    kj = lax.broadcasted_iota(jnp.int32, (nq, nk), 1)
    rel = jnp.abs(kj - ATTN_HALF - qi)
    return jnp.where(rel <= ATTN_HALF, (-LOG2E * slope_d) * rel.astype(F32), MASK_VALUE)


def _band_scores(qs, ks, bias, key_lo, key_hi):
    s = lax.dot_general(qs.astype(BF16), ks.astype(BF16), (((1,), (1,)), ((), ())),
                        preferred_element_type=F32) + bias
    if key_lo is not None or key_hi is not None:
        kj = lax.broadcasted_iota(jnp.int32, s.shape, 1)
        inside = None
        if key_lo is not None:
            inside = kj >= key_lo
        if key_hi is not None:
            inside = (kj < key_hi) if inside is None else (inside & (kj < key_hi))
        s = jnp.where(inside, s, MASK_VALUE)
    m = jnp.max(s, axis=1, keepdims=True)
    return s, jnp.broadcast_to(m, (s.shape[0], LANES))


def _band_softmax_pv(s, m_b, vs):
    nq, nk = s.shape
    reps = [m_b] * (nk // LANES) + ([m_b[:, :nk % LANES]] if nk % LANES else [])
    m_wide = reps[0] if len(reps) == 1 else jnp.concatenate(reps, axis=1)
    p = jnp.exp2(s - m_wide).astype(BF16)
    v_ext = jnp.concatenate([vs.astype(BF16), jnp.ones((nk, LANES), BF16)], axis=1)
    r = jnp.dot(p, v_ext, preferred_element_type=F32)
    l_b = r[:, ATTN_HEAD_DIM:]
    return r[:, :ATTN_HEAD_DIM] / l_b, m_b * (1.0 / LOG2E) + jnp.log(l_b)


def _attn_kernel(slopes_ref, q_ref, kp_ref, kc_ref, kn_ref, vp_ref, vc_ref, vn_ref, g_ref, y_ref,
                 q4, k4, v4, acc1, lse1, acc2, lse2, acc3, lse3, ynat, sbuf, mbuf, *, tq):
    t = pl.program_id(1)
    half = ATTN_HALF
    dh = ATTN_HEAD_DIM
    halo = ATTN_HALO
    n4, h4 = tq // 4, halo // 4
    n16 = tq // 16
    sub = ATTN_SUB
    scale = LOG2E * ATTN_HEAD_DIM ** -0.5
    first_tile = t == 0
    last_tile = t == pl.num_programs(1) - 1
    slope = slopes_ref[pl.program_id(2)]

    for c in range(4):
        q4[c] = q_ref[0, pl.ds(c, n4, stride=4), :] * scale
        for dst, (p_ref, c_ref, n_ref) in ((k4, (kp_ref, kc_ref, kn_ref)),
                                           (v4, (vp_ref, vc_ref, vn_ref))):
            dst[c, 0:h4, :] = p_ref[0, pl.ds(c, h4, stride=4), :]
            dst[c, h4:h4 + n4, :] = c_ref[0, pl.ds(c, n4, stride=4), :]
            dst[c, h4 + n4:, :] = n_ref[0, pl.ds(c, h4, stride=4), :]

    def bounds(j, nblk, nsub):
        lo = jnp.where(first_tile, half, 0) if j == 0 else None
        hi = jnp.where(last_tile, nsub + half, nsub + 2 * half) if j == nblk - 1 else None
        return lo, hi

    def run_blocks(blocks, nsub, slope_d):
        nk = nsub + 2 * half
        bias = _band_bias(nsub, nk, slope_d)
        for i, (load_q, load_k, _, lo, hi, _) in enumerate(blocks):
            s, m_b = _band_scores(load_q(), load_k(), bias, lo, hi)
            sbuf[i, 0:nsub, 0:nk] = s
            mbuf[i, 0:nsub, :] = m_b
        for i, (_, _, load_v, _, _, store) in enumerate(blocks):
            store(*_band_softmax_pv(sbuf[i, 0:nsub, 0:nk], mbuf[i, 0:nsub, :], load_v()))

    def window(refs, r0, nk):
        p_ref, c_ref, n_ref = refs
        parts = []
        if r0 < 0:
            parts.append(p_ref[0, halo + r0:halo, :])
        parts.append(c_ref[0, max(r0, 0):min(r0 + nk, tq), :])
        if r0 + nk > tq:
            parts.append(n_ref[0, 0:r0 + nk - tq, :])
        return parts[0] if len(parts) == 1 else jnp.concatenate(parts, axis=0)

    def store_to(acc, lse, *idx):
        def store(o, lse_b):
            acc[idx] = o
            lse[idx] = lse_b
        return store

    krefs, vrefs = (kp_ref, kc_ref, kn_ref), (vp_ref, vc_ref, vn_ref)
    nk = sub + 2 * half
    nblk = tq // sub
    blocks = []
    for j in range(nblk):
        rows = slice(j * sub, (j + 1) * sub)
        blocks.append((lambda rows=rows: q_ref[0, rows, :] * scale,
                       lambda j=j: window(krefs, j * sub - half, nk),
                       lambda j=j: window(vrefs, j * sub - half, nk),
                       *bounds(j, nblk, sub), store_to(acc1, lse1, rows, slice(None))))
    run_blocks(blocks, sub, slope)

    nblk = n4 // sub
    blocks = []
    for c in range(4):
        for j in range(nblk):
            rows = slice(j * sub, (j + 1) * sub)
            krows = slice(h4 + j * sub - half, h4 + j * sub - half + nk)
            blocks.append((lambda c=c, rows=rows: q4[c, rows, :],
                           lambda c=c, krows=krows: k4[c, krows, :],
                           lambda c=c, krows=krows: v4[c, krows, :],
                           *bounds(j, nblk, sub), store_to(acc2, lse2, c, rows, slice(None))))
    run_blocks(blocks, sub, slope * 4)

    nsub = min(sub, n16)
    nk = nsub + 2 * half
    nblk = n16 // nsub
    blocks = []
    for c in range(4):
        for c16 in range(4):
            for j in range(nblk):
                rows_q = pl.ds(c16 + 4 * j * nsub, nsub, stride=4)
                rows_k = pl.ds(h4 + c16 + 4 * (j * nsub - half), nk, stride=4)
                blocks.append((lambda c=c, rows_q=rows_q: q4[c, rows_q, :],
                               lambda c=c, rows_k=rows_k: k4[c, rows_k, :],
                               lambda c=c, rows_k=rows_k: v4[c, rows_k, :],
                               *bounds(j, nblk, nsub),
                               store_to(acc3, lse3, c, rows_q, slice(None))))
    run_blocks(blocks, nsub, slope * 16)

    for c in range(4):
        rows = pl.ds(c, n4, stride=4)
        l1, l2, l3 = lse1[rows, :], lse2[c], lse3[c]
        mx = jnp.maximum(jnp.maximum(l1, l2), l3)
        e1, e2, e3 = jnp.exp(l1 - mx), jnp.exp(l2 - mx), jnp.exp(l3 - mx)
        o = (e1 * acc1[rows, :] + e2 * acc2[c] + e3 * acc3[c]) / (e1 + e2 + e3)
        on = o * lax.rsqrt(jnp.mean(o * o, axis=-1, keepdims=True) + RMS_EPS)
        ynat[rows, :] = on * g_ref[...]
    y_ref[0] = ynat[...].astype(y_ref.dtype)


def _dilated_attention(slopes, qkv, head_g):
    bsz, s, width3 = qkv.shape
    width = width3 // 3
    tq = min(ATTN_TILE, s)
    halo = ATTN_HALO
    assert tq % halo == 0 and s % tq == 0 and (tq // 16) % min(ATTN_SUB, tq // 16) == 0
    dh = ATTN_HEAD_DIM
    nh = width // dh
    nhb = tq // halo
    last_hb = s // halo - 1
    n4, h4 = tq // 4, halo // 4
    cur = lambda o: pl.BlockSpec((1, tq, dh), lambda b, t, c, sl: (b, t, o + c))
    prev = lambda o: pl.BlockSpec((1, halo, dh),
                                  lambda b, t, c, sl: (b, jnp.maximum(t * nhb - 1, 0), o + c))
    nxt = lambda o: pl.BlockSpec((1, halo, dh),
                                 lambda b, t, c, sl: (b, jnp.minimum((t + 1) * nhb, last_hb), o + c))
    kern = functools.partial(_attn_kernel, tq=tq)
    nstage = max(tq // ATTN_SUB, 16 * ((tq // 16) // min(ATTN_SUB, tq // 16)))
    win = pltpu.VMEM((4, n4 + 2 * h4, dh), F32)
    cls = pltpu.VMEM((4, n4, dh), F32)
    nat = pltpu.VMEM((tq, dh), F32)
    return pl.pallas_call(
        kern,
        grid_spec=pltpu.PrefetchScalarGridSpec(
            num_scalar_prefetch=1,
            grid=(bsz, s // tq, width // dh),
            in_specs=[cur(0), prev(nh), cur(nh), nxt(nh), prev(2 * nh), cur(2 * nh), nxt(2 * nh),
                      pl.BlockSpec((1, dh), lambda b, t, c, sl: (0, c))],
            out_specs=pl.BlockSpec((1, tq, dh), lambda b, t, c, sl: (b, t, c)),
            scratch_shapes=[cls, win, win, nat, nat, cls, cls, cls, cls, nat,
                            pltpu.VMEM((nstage, ATTN_SUB, ATTN_SUB + 2 * ATTN_HALF), F32),
                            pltpu.VMEM((nstage, ATTN_SUB, LANES), F32)]),
        out_shape=jax.ShapeDtypeStruct((bsz, s, width), BF16),
        compiler_params=_params(("parallel", "parallel", "parallel")),
        name="dilated_attn",
    )(slopes, qkv, qkv, qkv, qkv, qkv, qkv, qkv, head_g.reshape(1, width))


def _start_row_gather(idx_ref, src_hbm, dst, sem, nrows):
    def body(r, carry):
        pltpu.make_async_copy(src_hbm.at[pl.ds(idx_ref[0, 0, r], 1), :],
                              dst.at[pl.ds(r, 1), :], sem).start()
        return carry

    lax.fori_loop(0, nrows, body, 0, unroll=8)


def _wait_row_gather(src_hbm, dst, sem, nrows):
    pltpu.make_async_copy(src_hbm.at[pl.ds(0, nrows), :], dst, sem).wait()


def _round_weights(src_ref, dst_ref):
    rows_total = dst_ref.shape[0]
    step = min(WEIGHT_CAST_ROWS, rows_total)

    def body(c, carry):
        rows = pl.ds(pl.multiple_of(c * step, step), step)
        dst_ref[rows, :] = src_ref[rows, :].astype(dst_ref.dtype)
        return carry

    lax.fori_loop(0, rows_total // step, body, 0)


def _expert_changed(be_ref, i):
    return (i == 0) | (be_ref[i] != be_ref[jnp.maximum(i - 1, 0)])


def _expert_up_kernel(be_ref, nb_ref, idx_ref, idx_next_ref, h_hbm, wg_ref, wu_ref, o_ref,
                      xbuf, sem, wgb_ref, wub_ref):
    i = pl.program_id(0)
    n_used = nb_ref[0]
    slot = i % 2
    blk = xbuf.shape[1]

    @pl.when(i == 0)
    def _():
        _start_row_gather(idx_ref, h_hbm, xbuf.at[0], sem.at[0], blk)

    @pl.when(i + 1 < n_used)
    def _():
        _start_row_gather(idx_next_ref, h_hbm, xbuf.at[1 - slot], sem.at[1 - slot], blk)

    @pl.when((i < n_used) & _expert_changed(be_ref, i))
    def _():
        _round_weights(wg_ref, wgb_ref)
        _round_weights(wu_ref, wub_ref)

    @pl.when(i < n_used)
    def _():
        _wait_row_gather(h_hbm, xbuf.at[slot], sem.at[slot], blk)
        x = xbuf[slot].astype(BF16)
        g = jnp.dot(x, wgb_ref[...], preferred_element_type=F32)
        u = jnp.dot(x, wub_ref[...], preferred_element_type=F32)
        o_ref[...] = (g * jax.nn.sigmoid(g) * u).astype(o_ref.dtype)

    @pl.when(i >= n_used)
    def _():
        o_ref[...] = jnp.zeros_like(o_ref)


HI16 = 0xFFFF0000


def _pack_bf16_pairs(lo, hi):
    bits = lambda v: lax.bitcast_convert_type(v.astype(BF16).astype(F32), jnp.uint32)
    return (bits(hi) & jnp.uint32(HI16)) | (bits(lo) >> 16)


def _unpack_bf16_pairs(p):
    lo = lax.bitcast_convert_type(p << 16, F32)
    hi = lax.bitcast_convert_type(p & jnp.uint32(HI16), F32)
    return lo, hi


def _expert_down_kernel(be_ref, nb_ref, h_ref, wd_ref, o_ref, wdb_ref):
    i = pl.program_id(0)
    half = o_ref.shape[1]

    @pl.when((i < nb_ref[0]) & _expert_changed(be_ref, i))
    def _():
        _round_weights(wd_ref, wdb_ref)

    @pl.when(i < nb_ref[0])
    def _():
        y = jnp.dot(h_ref[...], wdb_ref[...], preferred_element_type=F32)
        o_ref[...] = _pack_bf16_pairs(y[:, :half], y[:, half:])

    @pl.when(i >= nb_ref[0])
    def _():
        o_ref[...] = jnp.zeros_like(o_ref)


def _expert_ffn(h, slot_tok, block_e, n_used, w_gate, w_up, w_down, layer):
    d = h.shape[1]
    p = slot_tok.shape[0]
    ff = w_gate.shape[3]
    blk = EXPERT_BLOCK
    nblk = p // blk
    idx = slot_tok.reshape(nblk, 1, blk)
    h1 = pl.pallas_call(
        _expert_up_kernel,
        grid_spec=pltpu.PrefetchScalarGridSpec(
            num_scalar_prefetch=2,
            grid=(nblk,),
            in_specs=[pl.BlockSpec((1, 1, blk), lambda i, be, nb: (i, 0, 0),
                                   memory_space=pltpu.SMEM),
                      pl.BlockSpec((1, 1, blk), lambda i, be, nb: (jnp.minimum(i + 1, nblk - 1), 0, 0),
                                   memory_space=pltpu.SMEM),
                      pl.BlockSpec(memory_space=pl.ANY),
                      pl.BlockSpec((None, None, d, ff), lambda i, be, nb: (layer, be[i], 0, 0),
                                   pipeline_mode=pl.Buffered(1)),
                      pl.BlockSpec((None, None, d, ff), lambda i, be, nb: (layer, be[i], 0, 0),
                                   pipeline_mode=pl.Buffered(1))],
            out_specs=pl.BlockSpec((blk, ff), lambda i, be, nb: (i, 0)),
            scratch_shapes=[pltpu.VMEM((2, blk, d), F32), pltpu.SemaphoreType.DMA((2,)),
                            pltpu.VMEM((d, ff), BF16), pltpu.VMEM((d, ff), BF16)]),
        out_shape=jax.ShapeDtypeStruct((p, ff), BF16),
        compiler_params=_params(("arbitrary",)),
        name="expert_up",
    )(block_e, n_used, idx, idx, h, w_gate, w_up)
    return pl.pallas_call(
        _expert_down_kernel,
        grid_spec=pltpu.PrefetchScalarGridSpec(
            num_scalar_prefetch=2,
            grid=(nblk,),
            in_specs=[pl.BlockSpec((blk, ff), lambda i, be, nb: (i, 0)),
                      pl.BlockSpec((None, None, ff, d), lambda i, be, nb: (layer, be[i], 0, 0))],
            out_specs=pl.BlockSpec((blk, d // 2), lambda i, be, nb: (i, 0)),
            scratch_shapes=[pltpu.VMEM((ff, d), BF16)]),
        out_shape=jax.ShapeDtypeStruct((p, d // 2), jnp.uint32),
        compiler_params=_params(("arbitrary",)),
        name="expert_down",
    )(block_e, n_used, h1, w_down)


def _combine_kernel(p0_ref, p1_ref, p0n_ref, p1n_ref, x_ref, w0_ref, w1_ref, g_ref, y_hbm,
                    *rest, emit_x):
    if emit_x:
        xo_ref, ho_ref, buf, sem = rest
    else:
        ho_ref, buf, sem = rest
    i = pl.program_id(0)
    nt = pl.num_programs(0)
    slot = i % 2
    tm = buf.shape[2]

    @pl.when(i == 0)
    def _():
        _start_row_gather(p0_ref, y_hbm, buf.at[0, 0], sem.at[0], tm)
        _start_row_gather(p1_ref, y_hbm, buf.at[0, 1], sem.at[0], tm)

    @pl.when(i + 1 < nt)
    def _():
        _start_row_gather(p0n_ref, y_hbm, buf.at[1 - slot, 0], sem.at[1 - slot], tm)
        _start_row_gather(p1n_ref, y_hbm, buf.at[1 - slot, 1], sem.at[1 - slot], tm)

    _wait_row_gather(y_hbm, buf.at[slot, 0], sem.at[slot], tm)
    _wait_row_gather(y_hbm, buf.at[slot, 1], sem.at[slot], tm)
    half = buf.shape[3]
    lo0, hi0 = _unpack_bf16_pairs(buf[slot, 0])
    lo1, hi1 = _unpack_bf16_pairs(buf[slot, 1])
    w0, w1 = w0_ref[...], w1_ref[...]
    x_lo = x_ref[:, :half] + w0 * lo0 + w1 * lo1
    x_hi = x_ref[:, half:] + w0 * hi0 + w1 * hi1
    if emit_x:
        xo_ref[:, :half] = x_lo
        xo_ref[:, half:] = x_hi
    ssq = (jnp.sum(x_lo * x_lo, axis=-1, keepdims=True)
           + jnp.sum(x_hi * x_hi, axis=-1, keepdims=True))
    inv = lax.rsqrt(ssq * (1.0 / (2 * half)) + RMS_EPS)
    ho_ref[:, :half] = (x_lo * inv * g_ref[:, :half]).astype(ho_ref.dtype)
    ho_ref[:, half:] = (x_hi * inv * g_ref[:, half:]).astype(ho_ref.dtype)


def _combine_norm(x2d, y, pos, wts, g, emit_x, tm=256):
    t, d = x2d.shape
    nt = t // tm
    pidx = pos.reshape(EXPERT_TOPK, nt, 1, tm)
    wcol = wts.reshape(EXPERT_TOPK, t, 1)
    cur = lambda i: (i, 0, 0)
    nxt = lambda i: (jnp.minimum(i + 1, nt - 1), 0, 0)
    ispec = lambda f: pl.BlockSpec((1, 1, tm), f, memory_space=pltpu.SMEM)
    row = pl.BlockSpec((tm, d), lambda i: (i, 0))
    wspec = pl.BlockSpec((tm, 1), lambda i: (i, 0))
    if emit_x:
        out_specs = [row, row]
        out_shape = [jax.ShapeDtypeStruct((t, d), F32), jax.ShapeDtypeStruct((t, d), BF16)]
    else:
        out_specs = row
        out_shape = jax.ShapeDtypeStruct((t, d), F32)
    return pl.pallas_call(
        functools.partial(_combine_kernel, emit_x=emit_x),
        grid=(nt,),
        in_specs=[ispec(cur), ispec(cur), ispec(nxt), ispec(nxt), row, wspec, wspec,
                  pl.BlockSpec((1, d), lambda i: (0, 0)),
                  pl.BlockSpec(memory_space=pl.ANY)],
        out_specs=out_specs,
        out_shape=out_shape,
        scratch_shapes=[pltpu.VMEM((2, EXPERT_TOPK, tm, d // 2), jnp.uint32),
                        pltpu.SemaphoreType.DMA((2,))],
        compiler_params=_params(("arbitrary",)),
        name="moe_combine_norm",
    )(pidx[0], pidx[1], pidx[0], pidx[1], x2d, wcol[0], wcol[1], g.reshape(1, d), y)


def _route(eid, n_experts):
    t = eid.shape[1]
    a = EXPERT_TOPK * t
    flat_e = eid.reshape(a)
    onehot = (flat_e[:, None] == jnp.arange(n_experts, dtype=jnp.int32)[None, :]).astype(jnp.int32)
    csum = jnp.cumsum(onehot, axis=0)
    counts = csum[-1]
    blocks = (counts + EXPERT_BLOCK - 1) // EXPERT_BLOCK
    blk_end = jnp.cumsum(blocks)
    pad_start = (blk_end - blocks) * EXPERT_BLOCK
    dest = jnp.sum(onehot * (csum - 1 + pad_start[None, :]), axis=1)
    n_blocks = -(-a // EXPERT_BLOCK) + n_experts
    p = n_blocks * EXPERT_BLOCK
    tok = jnp.arange(a, dtype=jnp.int32) % t
    slot_tok = jnp.zeros((p,), jnp.int32).at[dest].set(tok)
    block_e = jnp.sum((blk_end[None, :] <= jnp.arange(n_blocks, dtype=jnp.int32)[:, None])
                      .astype(jnp.int32), axis=1)
    block_e = jnp.minimum(block_e, n_experts - 1)
    n_used = blk_end[-1:].astype(jnp.int32)
    return slot_tok, block_e, n_used, dest.reshape(EXPERT_TOPK, t)


def kernel(x, norm1_g, w_in, b_gate, conv_w, conv_b, head_norm_g, w_out, norm2_g, w_router_group,
           b_router_group, w_router_expert, b_router_expert, w_gate, w_up, w_down, final_norm_g):
    depth = norm1_g.shape[0]
    bsz, s, d = x.shape
    t = bsz * s
    mw = d // 2
    nmh = mw // MLSTM_DV
    qkw = nmh * MLSTM_DK
    aw = d - mw
    nah = aw // ATTN_HEAD_DIM
    ngate = 4 * nmh
    n_groups = w_router_group.shape[2]
    n_experts = w_router_expert.shape[2]

    c_v = 2 * qkw
    c_o = c_v + mw
    c_g = c_o + mw
    c_a = c_g + ngate
    assert ngate <= LANES and c_a + 3 * aw == w_in.shape[2]
    w_in_t = jnp.swapaxes(w_in, 1, 2)
    npad = ROUTER_ROWS - n_groups - n_experts
    w_router_t = jnp.pad(jnp.concatenate([w_router_group, w_router_expert], axis=2)
                         .transpose(0, 2, 1), ((0, 0), (0, npad), (0, 0)))
    b_router_t = jnp.pad(jnp.concatenate([b_router_group, b_router_expert], axis=1),
                         ((0, 0), (0, npad))).reshape(depth, ROUTER_ROWS, 1)
    slopes = jnp.exp2(-8.0 * jnp.arange(1, nah + 1, dtype=F32) / nah)

    x2d = x.reshape(t, d)
    h = _rmsnorm(x2d, norm1_g[0], BF16)
    for l in range(depth):
        head_g = head_norm_g[l]
        z_m = _matmul(h, w_in_t, l, F32, row0=0, n=c_g).reshape(bsz, s, c_g)
        gates = _matmul(h, w_in_t, l, F32, row0=c_g, n=LANES)[:, :ngate]
        z_a = _matmul(h, w_in_t, l, F32, row0=c_a, n=3 * aw).reshape(bsz, s, 3 * aw)

        q_m = _short_conv_silu(z_m, conv_w[l], conv_b[l], 0, qkw, MLSTM_DK ** -0.5,
                               transpose=False)
        kt_m = _short_conv_silu(z_m, conv_w[l], conv_b[l], qkw, qkw, 1.0, transpose=True)
        gates_t = gates.reshape(bsz, s, 4, nmh).transpose(0, 2, 3, 1).reshape(bsz, 4, nmh, 1, s)
        h_bwd = _mlstm_scan(b_gate[l], q_m, kt_m, z_m, 1, gates_t, reverse=True)
        y_m = _mlstm_scan(b_gate[l], q_m, kt_m, z_m, 1, gates_t, reverse=False, hprev=h_bwd,
                          o_blk=2, head_g=head_g[:mw].reshape(1, mw))

        y_a = _dilated_attention(slopes, z_a, head_g[mw:])

        x2d = _out_proj(y_m.reshape(t, mw), y_a.reshape(t, aw), w_out, l, x2d)

        h2, eid, wts = _rmsnorm_router(x2d, norm2_g[l], w_router_t[l], b_router_t[l],
                                       n_groups, n_experts // n_groups)
        slot_tok, block_e, n_used, pos = _route(eid[:EXPERT_TOPK], n_experts)
        yb = _expert_ffn(h2, slot_tok, block_e, n_used, w_gate, w_up, w_down, l)
        if l + 1 < depth:
            x2d, h = _combine_norm(x2d, yb, pos, wts[:EXPERT_TOPK], norm1_g[l + 1], emit_x=True)
        else:
            out = _combine_norm(x2d, yb, pos, wts[:EXPERT_TOPK], final_norm_g, emit_x=False)
    return out.reshape(bsz, s, d)
```

```python
import functools

import jax
import jax.numpy as jnp
from jax import lax
from jax.experimental import pallas as pl
from jax.experimental.pallas import tpu as pltpu

MLSTM_DV = 512
MLSTM_DK = 256
ATTN_HEAD_DIM = 128
CONV_W = 5
DILATED_PATTERNS = ((128, 1), (512, 4), (2048, 16))
ATTN_HALF = 64
EXPERT_TOPK = 2
RMS_EPS = 1e-6
MASK_VALUE = -1e30
LOG2E = 1.4426950408889634

LANES = 128
SUBLANES = 8
VMEM_LIMIT_BYTES = 56 * 1024 * 1024
PROJ_VMEM_LIMIT_BYTES = 60 * 1024 * 1024

MLSTM_CHUNK = 256
EXPERT_BLOCK = 256
ROUTER_ROWS = 32

BF16 = jnp.bfloat16
F32 = jnp.float32


def _params(semantics):
    return pltpu.CompilerParams(dimension_semantics=semantics, vmem_limit_bytes=VMEM_LIMIT_BYTES)


def _rmsnorm_kernel(x_ref, g_ref, o_ref):
    x = x_ref[...]
    ms = jnp.mean(x * x, axis=-1, keepdims=True)
    o_ref[...] = (x * lax.rsqrt(ms + RMS_EPS) * g_ref[...]).astype(o_ref.dtype)


def _rmsnorm(x2d, g, out_dtype, tm=256):
    t, d = x2d.shape
    return pl.pallas_call(
        _rmsnorm_kernel,
        grid=(t // tm,),
        in_specs=[pl.BlockSpec((tm, d), lambda i: (i, 0)),
                  pl.BlockSpec((1, d), lambda i: (0, 0))],
        out_specs=pl.BlockSpec((tm, d), lambda i: (i, 0)),
        out_shape=jax.ShapeDtypeStruct((t, d), out_dtype),
        compiler_params=_params(("parallel",)),
        name="rmsnorm",
    )(x2d, g.reshape(1, d))


def _first_argmax(vals):
    best = vals[0]
    idx = jnp.zeros(best.shape, jnp.int32)
    for i in range(1, len(vals)):
        gt = vals[i] > best
        best = jnp.where(gt, vals[i], best)
        idx = jnp.where(gt, i, idx)
    return best, idx


def _softmax_list(vals):
    mx = functools.reduce(jnp.maximum, vals)
    es = [jnp.exp(v - mx) for v in vals]
    tot = functools.reduce(lambda a, b: a + b, es)
    return [e / tot for e in es]


def _rmsnorm_router_kernel(x_ref, g_ref, wrt_ref, brt_ref, h_ref, eid_ref, wt_ref,
                           *, n_groups, per_group):
    x = x_ref[...]
    ms = jnp.mean(x * x, axis=-1, keepdims=True)
    h = x * lax.rsqrt(ms + RMS_EPS) * g_ref[...]
    h_ref[...] = h.astype(h_ref.dtype)
    def split(v):
        hi = v.astype(BF16)
        return hi, (v - hi.astype(F32)).astype(BF16)

    def dot_nt(a, b):
        return lax.dot_general(a, b, (((1,), (1,)), ((), ())), preferred_element_type=F32)

    w_hi, w_lo = split(wrt_ref[...])
    h_hi, h_lo = split(h)
    lt = dot_nt(w_hi, h_hi) + dot_nt(w_lo, h_hi) + dot_nt(w_hi, h_lo) + brt_ref[...]
    g_prob = _softmax_list([lt[i:i + 1, :] for i in range(n_groups)])
    g_top_p, g_top = _first_argmax(g_prob)
    e_sel = []
    for j in range(per_group):
        row = lt[n_groups + j:n_groups + j + 1, :]
        for grp in range(1, n_groups):
            r = n_groups + grp * per_group + j
            row = jnp.where(g_top == grp, lt[r:r + 1, :], row)
        e_sel.append(row)
    e_prob = _softmax_list(e_sel)
    p1, i1 = _first_argmax(e_prob)
    p2, i2 = _first_argmax([jnp.where(i1 == j, -1.0, e_prob[j]) for j in range(per_group)])
    denom = p1 + p2
    w1 = g_top_p * p1 / denom
    w2 = g_top_p * p2 / denom
    e1 = g_top * per_group + i1
    e2 = g_top * per_group + i2
    rows = lax.broadcasted_iota(jnp.int32, eid_ref.shape, 0)
    eid_ref[...] = jnp.where(rows == 0, e1, jnp.where(rows == 1, e2, 0))
    wt_ref[...] = jnp.where(rows == 0, w1, jnp.where(rows == 1, w2, 0.0))


def _rmsnorm_router(x2d, g, w_router_t, b_router_t, n_groups, per_group, tm=256):
    t, d = x2d.shape
    kern = functools.partial(_rmsnorm_router_kernel, n_groups=n_groups, per_group=per_group)
    return pl.pallas_call(
        kern,
        grid=(t // tm,),
        in_specs=[pl.BlockSpec((tm, d), lambda i: (i, 0)),
                  pl.BlockSpec((1, d), lambda i: (0, 0)),
                  pl.BlockSpec((ROUTER_ROWS, d), lambda i: (0, 0)),
                  pl.BlockSpec((ROUTER_ROWS, 1), lambda i: (0, 0))],
        out_specs=[pl.BlockSpec((tm, d), lambda i: (i, 0)),
                   pl.BlockSpec((SUBLANES, tm), lambda i: (0, i)),
                   pl.BlockSpec((SUBLANES, tm), lambda i: (0, i))],
        out_shape=[jax.ShapeDtypeStruct((t, d), F32),
                   jax.ShapeDtypeStruct((SUBLANES, t), jnp.int32),
                   jax.ShapeDtypeStruct((SUBLANES, t), F32)],
        compiler_params=_params(("parallel",)),
        name="rmsnorm_router",
    )(x2d, g.reshape(1, d), w_router_t, b_router_t)


WEIGHT_CAST_ROWS = 256


def _matmul_kernel(a_ref, bt_ref, o_ref, bb_ref):
    @pl.when(pl.program_id(1) == 0)
    def _():
        step = min(WEIGHT_CAST_ROWS, bb_ref.shape[0])

        def body(c, carry):
            rows = pl.ds(pl.multiple_of(c * step, step), step)
            bb_ref[rows, :] = bt_ref[0, rows, :].astype(BF16)
            return carry

        lax.fori_loop(0, bb_ref.shape[0] // step, body, 0)

    o_ref[...] = lax.dot_general(a_ref[...], bb_ref[...], (((1,), (1,)), ((), ())),
                                 preferred_element_type=F32).astype(o_ref.dtype)


def _matmul(a, bt, layer, out_dtype, row0, n, tm=512, tn=1024):
    m, k = a.shape
    tn = min(tn, n)
    tm = min(tm, m)
    assert row0 % SUBLANES == 0 and n % tn == 0 and m % tm == 0
    return pl.pallas_call(
        _matmul_kernel,
        grid=(n // tn, m // tm),
        in_specs=[pl.BlockSpec((tm, k), lambda j, i: (i, 0)),
                  pl.BlockSpec((pl.Element(1), pl.Element(tn), pl.Element(k)),
                               lambda j, i: (layer, pl.multiple_of(row0 + j * tn, SUBLANES), 0))],
        out_specs=pl.BlockSpec((tm, tn), lambda j, i: (i, j)),
        out_shape=jax.ShapeDtypeStruct((m, n), out_dtype),
        scratch_shapes=[pltpu.VMEM((tn, k), BF16)],
        compiler_params=pltpu.CompilerParams(dimension_semantics=("parallel", "arbitrary"),
                                             vmem_limit_bytes=PROJ_VMEM_LIMIT_BYTES),
        name="proj_in",
    )(a, bt)


def _out_proj_kernel(a1_ref, a2_ref, b1_ref, b2_ref, r_ref, o_ref, bb1_ref, bb2_ref):
    @pl.when(pl.program_id(1) == 0)
    def _():
        bb1_ref[...] = b1_ref[...].astype(BF16)
        bb2_ref[...] = b2_ref[...].astype(BF16)

    acc = jnp.dot(a1_ref[...], bb1_ref[...], preferred_element_type=F32)
    acc = acc + jnp.dot(a2_ref[...], bb2_ref[...], preferred_element_type=F32)
    o_ref[...] = r_ref[...] + acc


def _out_proj(a1, a2, w, layer, res, tm=512, tn=1024):
    m, k1 = a1.shape
    k2 = a2.shape[1]
    n = w.shape[2]
    tn = min(tn, n)
    tm = min(tm, m)
    assert k1 == k2
    return pl.pallas_call(
        _out_proj_kernel,
        grid=(n // tn, m // tm),
        in_specs=[pl.BlockSpec((tm, k1), lambda j, i: (i, 0)),
                  pl.BlockSpec((tm, k2), lambda j, i: (i, 0)),
                  pl.BlockSpec((None, k1, tn), lambda j, i: (layer, 0, j)),
                  pl.BlockSpec((None, k2, tn), lambda j, i: (layer, 1, j)),
                  pl.BlockSpec((tm, tn), lambda j, i: (i, j))],
        out_specs=pl.BlockSpec((tm, tn), lambda j, i: (i, j)),
        out_shape=jax.ShapeDtypeStruct((m, n), F32),
        scratch_shapes=[pltpu.VMEM((k1, tn), BF16), pltpu.VMEM((k2, tn), BF16)],
        compiler_params=pltpu.CompilerParams(dimension_semantics=("parallel", "arbitrary"),
                                             vmem_limit_bytes=PROJ_VMEM_LIMIT_BYTES),
        name="proj_out",
    )(a1, a2, w, w, res)


def _conv_kernel(up_ref, uc_ref, un_ref, w_ref, b_ref, o_ref, buf_ref, *, ts, scale, transpose):
    t = pl.program_id(2)
    nt = pl.num_programs(2)
    pad = SUBLANES
    buf_ref[0:pad, :] = jnp.where(t > 0, up_ref[0], 0.0)
    buf_ref[pad:pad + ts, :] = uc_ref[0]
    buf_ref[pad + ts:2 * pad + ts, :] = jnp.where(t < nt - 1, un_ref[0], 0.0)
    acc = jnp.zeros((ts, buf_ref.shape[1]), F32) + b_ref[...]
    for j in range(CONV_W):
        acc = acc + buf_ref[pl.ds(pad - CONV_W // 2 + j, ts), :] * w_ref[j:j + 1, :]
    y = acc * jax.nn.sigmoid(acc) * scale
    if transpose:
        o_ref[0] = y.T.astype(o_ref.dtype)
    else:
        o_ref[0] = y.astype(o_ref.dtype)


def _short_conv_silu(u, conv_w, conv_b, col0, width, scale, transpose, ts=512):
    bsz, s, _ = u.shape
    tc = MLSTM_DK
    ts = min(ts, s)
    cb0 = col0 // tc
    nrb = ts // SUBLANES
    last_rb = s // SUBLANES - 1
    kern = functools.partial(_conv_kernel, ts=ts, scale=scale, transpose=transpose)
    if transpose:
        out_spec = pl.BlockSpec((1, tc, ts), lambda b, c, t: (b, c, t))
        out_shape = jax.ShapeDtypeStruct((bsz, width, s), BF16)
    else:
        out_spec = pl.BlockSpec((1, ts, tc), lambda b, c, t: (b, t, c))
        out_shape = jax.ShapeDtypeStruct((bsz, s, width), BF16)
    return pl.pallas_call(
        kern,
        grid=(bsz, width // tc, s // ts),
        in_specs=[pl.BlockSpec((1, SUBLANES, tc),
                               lambda b, c, t: (b, jnp.maximum(t * nrb - 1, 0), cb0 + c)),
                  pl.BlockSpec((1, ts, tc), lambda b, c, t: (b, t, cb0 + c)),
                  pl.BlockSpec((1, SUBLANES, tc),
                               lambda b, c, t: (b, jnp.minimum((t + 1) * nrb, last_rb), cb0 + c)),
                  pl.BlockSpec((CONV_W, tc), lambda b, c, t: (0, cb0 + c)),
                  pl.BlockSpec((1, tc), lambda b, c, t: (0, cb0 + c))],
        out_specs=out_spec,
        out_shape=out_shape,
        scratch_shapes=[pltpu.VMEM((ts + 2 * SUBLANES, tc), F32)],
        compiler_params=_params(("parallel", "parallel", "parallel")),
        name="conv_silu_t" if transpose else "conv_silu",
    )(u, u, u, conv_w, conv_b.reshape(1, -1))


def _log_sigmoid(x):
    return jnp.minimum(x, 0.0) - jnp.log1p(jnp.exp(-jnp.abs(x)))


def _mlstm_kernel(bias_ref, q_ref, kt_ref, v_ref, g_ref, *rest, chunk, nheads, reverse, final):
    if final:
        hprev_ref, og_ref, hg_ref, o_ref, ct_ref, m_ref, vext_ref, wi_ref, col_ref, row_ref = rest
    else:
        o_ref, ct_ref, m_ref, vext_ref, wi_ref, col_ref, row_ref = rest
    c_idx = pl.program_id(1)
    dv, dk = MLSTM_DV, MLSTM_DK
    ln = chunk

    @pl.when(c_idx == 0)
    def _():
        ct_ref[...] = jnp.zeros_like(ct_ref)
        m_ref[...] = jnp.zeros_like(m_ref)

    gi, gf = (2, 3) if reverse else (0, 1)
    row = lax.broadcasted_iota(jnp.int32, (ln, ln), 0)
    col = lax.broadcasted_iota(jnp.int32, (ln, ln), 1)
    cum_mask = ((row >= col) if reverse else (row <= col)).astype(F32)
    tri = (col >= row) if reverse else (col <= row)
    diag = row == col
    ones_cols = jnp.ones((ln, LANES), vext_ref.dtype)

    i_rows = [g_ref[gi, hh] + bias_ref[gi * nheads + hh] for hh in range(nheads)]
    f_rows = [_log_sigmoid(g_ref[gf, hh] + bias_ref[gf * nheads + hh]) for hh in range(nheads)]
    f_all = jnp.concatenate(f_rows + [jnp.zeros((SUBLANES - nheads, ln), F32)], axis=0)
    b_all = jnp.dot(f_all, cum_mask, precision=lax.Precision.HIGHEST, preferred_element_type=F32)
    for hh in range(nheads):
        i_row = i_rows[hh]
        b_row = b_all[hh:hh + 1, :]
        total = b_row[:, 0:1] if reverse else b_row[:, ln - 1:ln]
        b_col = jnp.sum(jnp.where(diag, b_row, 0.0), axis=1, keepdims=True)
        m0 = m_ref[hh, 0:1, 0:1]
        d_log = jnp.where(tri, b_col - b_row + i_row, MASK_VALUE)
        a_col = b_col + m0
        m_t = jnp.maximum(a_col, jnp.max(d_log, axis=1, keepdims=True))
        wi_ref[hh] = jnp.exp(d_log - m_t)
        col_ref[hh, :, 0:1] = jnp.exp(a_col - m_t)
        col_ref[hh, :, 1:2] = jnp.exp(-m_t)
        g_row = total - b_row + i_row
        m_new = jnp.maximum(total + m0, jnp.max(g_row, axis=1, keepdims=True))
        row_ref[hh, 0:1, :] = jnp.exp(g_row - m_new)
        row_ref[hh, 1:2, :] = jnp.broadcast_to(jnp.exp(total + m0 - m_new), (1, ln))
        m_ref[hh] = jnp.broadcast_to(m_new, m_ref.shape[1:])

    for hh in range(nheads):
        w_inter = col_ref[hh, :, 0:1]
        inv_floor = col_ref[hh, :, 1:2]
        w_add = row_ref[hh, 0:1, :]
        decay = row_ref[hh, 1:2, 0:1]
        q = q_ref[0, :, hh * dk:(hh + 1) * dk]
        kt = kt_ref[0, hh * dk:(hh + 1) * dk, :]
        vext_ref[hh, :, 0:dv] = v_ref[0, :, hh * dv:(hh + 1) * dv].astype(vext_ref.dtype)
        vext_ref[hh, :, dv:] = ones_cols
        vext = vext_ref[hh]

        s_qk = jnp.dot(q, kt, preferred_element_type=F32) * wi_ref[hh]
        intra = jnp.dot(s_qk.astype(BF16), vext, preferred_element_type=F32)
        ct = ct_ref[hh]
        inter = jnp.dot(q, ct.astype(BF16), preferred_element_type=F32)
        tot = intra + w_inter * inter
        den = tot[:, dv:dv + 1]
        h = tot[:, 0:dv] / jnp.maximum(jnp.abs(den), inv_floor)

        ktw = (kt.astype(F32) * w_add).astype(BF16)
        ct_ref[hh] = decay * ct + jnp.dot(ktw, vext, preferred_element_type=F32)

        cols = slice(hh * dv, (hh + 1) * dv)
        if final:
            hs = h + hprev_ref[0, :, cols]
            hn = hs * lax.rsqrt(jnp.mean(hs * hs, axis=-1, keepdims=True) + RMS_EPS)
            o_ref[0, :, cols] = (hn * hg_ref[:, cols]
                                 * jax.nn.sigmoid(og_ref[0, :, cols])).astype(o_ref.dtype)
        else:
            o_ref[0, :, cols] = h


def _mlstm_scan(bias, q, kt, z, v_blk, gates_t, reverse, hprev=None, o_blk=None, head_g=None):
    bsz, s, qkw = q.shape
    nheads = qkw // MLSTM_DK
    ln = min(MLSTM_CHUNK, s)
    nc = s // ln
    final = hprev is not None

    def cidx(c):
        return (nc - 1 - c) if reverse else c

    vw = nheads * MLSTM_DV
    wide = lambda blk: pl.BlockSpec((1, ln, vw), lambda b, c: (b, cidx(c), blk))
    in_specs = [pl.BlockSpec(memory_space=pltpu.SMEM),
                pl.BlockSpec((1, ln, qkw), lambda b, c: (b, cidx(c), 0)),
                pl.BlockSpec((1, qkw, ln), lambda b, c: (b, 0, cidx(c))),
                wide(v_blk),
                pl.BlockSpec((None, 4, nheads, 1, ln), lambda b, c: (b, 0, 0, 0, cidx(c)))]
    args = [bias, q, kt, z, gates_t]
    if final:
        in_specs += [wide(0), wide(o_blk), pl.BlockSpec((1, vw), lambda b, c: (0, 0))]
        args += [hprev, z, head_g]
    kern = functools.partial(_mlstm_kernel, chunk=ln, nheads=nheads, reverse=reverse, final=final)
    return pl.pallas_call(
        kern,
        grid=(bsz, nc),
        in_specs=in_specs,
        out_specs=wide(0),
        out_shape=jax.ShapeDtypeStruct((bsz, s, vw), BF16 if final else F32),
        scratch_shapes=[pltpu.VMEM((nheads, MLSTM_DK, MLSTM_DV + LANES), F32),
                        pltpu.VMEM((nheads, SUBLANES, LANES), F32),
                        pltpu.VMEM((nheads, ln, MLSTM_DV + LANES), BF16),
                        pltpu.VMEM((nheads, ln, ln), F32),
                        pltpu.VMEM((nheads, ln, LANES), F32),
                        pltpu.VMEM((nheads, SUBLANES, ln), F32)],
        compiler_params=_params(("parallel", "arbitrary")),
        name="mlstm_fwd" if final else "mlstm_bwd",
    )(*args)


ATTN_TILE = 4096
ATTN_HALO = ATTN_HALF * 16
ATTN_SUB = 128


def _band_bias(nq, nk, slope_d):
    qi = lax.broadcasted_iota(jnp.int32, (nq, nk), 0)
    kj = lax.broadcasted_iota(jnp.int32, (nq, nk), 1)
    rel = jnp.abs(kj - ATTN_HALF - qi)
    return jnp.where(rel <= ATTN_HALF, (-LOG2E * slope_d) * rel.astype(F32), MASK_VALUE)


def _band_scores(qs, ks, bias, key_lo, key_hi):
    s = lax.dot_general(qs.astype(BF16), ks.astype(BF16), (((1,), (1,)), ((), ())),
                        preferred_element_type=F32) + bias
    if key_lo is not None or key_hi is not None:
        kj = lax.broadcasted_iota(jnp.int32, s.shape, 1)
        inside = None
        if key_lo is not None:
            inside = kj >= key_lo
        if key_hi is not None:
            inside = (kj < key_hi) if inside is None else (inside & (kj < key_hi))
        s = jnp.where(inside, s, MASK_VALUE)
    m = jnp.max(s, axis=1, keepdims=True)
    return s, jnp.broadcast_to(m, (s.shape[0], LANES))


def _band_softmax_pv(s, m_b, vs):
    nq, nk = s.shape
    reps = [m_b] * (nk // LANES) + ([m_b[:, :nk % LANES]] if nk % LANES else [])
    m_wide = reps[0] if len(reps) == 1 else jnp.concatenate(reps, axis=1)
    p = jnp.exp2(s - m_wide).astype(BF16)
    v_ext = jnp.concatenate([vs.astype(BF16), jnp.ones((nk, LANES), BF16)], axis=1)
    r = jnp.dot(p, v_ext, preferred_element_type=F32)
    l_b = r[:, ATTN_HEAD_DIM:]
    return r[:, :ATTN_HEAD_DIM] / l_b, m_b * (1.0 / LOG2E) + jnp.log(l_b)


def _attn_kernel(slopes_ref, q_ref, kp_ref, kc_ref, kn_ref, vp_ref, vc_ref, vn_ref, g_ref, y_ref,
                 q4, k4, v4, acc1, lse1, acc2, lse2, acc3, lse3, ynat, sbuf, mbuf, *, tq):
    t = pl.program_id(1)
    half = ATTN_HALF
    dh = ATTN_HEAD_DIM
    halo = ATTN_HALO
    n4, h4 = tq // 4, halo // 4
    n16 = tq // 16
    sub = ATTN_SUB
    scale = LOG2E * ATTN_HEAD_DIM ** -0.5
    first_tile = t == 0
    last_tile = t == pl.num_programs(1) - 1
    slope = slopes_ref[pl.program_id(2)]

    for c in range(4):
        q4[c] = q_ref[0, pl.ds(c, n4, stride=4), :] * scale
        for dst, (p_ref, c_ref, n_ref) in ((k4, (kp_ref, kc_ref, kn_ref)),
                                           (v4, (vp_ref, vc_ref, vn_ref))):
            dst[c, 0:h4, :] = p_ref[0, pl.ds(c, h4, stride=4), :]
            dst[c, h4:h4 + n4, :] = c_ref[0, pl.ds(c, n4, stride=4), :]
            dst[c, h4 + n4:, :] = n_ref[0, pl.ds(c, h4, stride=4), :]

    def bounds(j, nblk, nsub):
        lo = jnp.where(first_tile, half, 0) if j == 0 else None
        hi = jnp.where(last_tile, nsub + half, nsub + 2 * half) if j == nblk - 1 else None
        return lo, hi

    def run_blocks(blocks, nsub, slope_d):
        nk = nsub + 2 * half
        bias = _band_bias(nsub, nk, slope_d)
        for i, (load_q, load_k, _, lo, hi, _) in enumerate(blocks):
            s, m_b = _band_scores(load_q(), load_k(), bias, lo, hi)
            sbuf[i, 0:nsub, 0:nk] = s
            mbuf[i, 0:nsub, :] = m_b
        for i, (_, _, load_v, _, _, store) in enumerate(blocks):
            store(*_band_softmax_pv(sbuf[i, 0:nsub, 0:nk], mbuf[i, 0:nsub, :], load_v()))

    def window(refs, r0, nk):
        p_ref, c_ref, n_ref = refs
        parts = []
        if r0 < 0:
            parts.append(p_ref[0, halo + r0:halo, :])
        parts.append(c_ref[0, max(r0, 0):min(r0 + nk, tq), :])
        if r0 + nk > tq:
            parts.append(n_ref[0, 0:r0 + nk - tq, :])
        return parts[0] if len(parts) == 1 else jnp.concatenate(parts, axis=0)

    def store_to(acc, lse, *idx):
        def store(o, lse_b):
            acc[idx] = o
            lse[idx] = lse_b
        return store

    krefs, vrefs = (kp_ref, kc_ref, kn_ref), (vp_ref, vc_ref, vn_ref)
    nk = sub + 2 * half
    nblk = tq // sub
    blocks = []
    for j in range(nblk):
        rows = slice(j * sub, (j + 1) * sub)
        blocks.append((lambda rows=rows: q_ref[0, rows, :] * scale,
                       lambda j=j: window(krefs, j * sub - half, nk),
                       lambda j=j: window(vrefs, j * sub - half, nk),
                       *bounds(j, nblk, sub), store_to(acc1, lse1, rows, slice(None))))
    run_blocks(blocks, sub, slope)

    nblk = n4 // sub
    blocks = []
    for c in range(4):
        for j in range(nblk):
            rows = slice(j * sub, (j + 1) * sub)
            krows = slice(h4 + j * sub - half, h4 + j * sub - half + nk)
            blocks.append((lambda c=c, rows=rows: q4[c, rows, :],
                           lambda c=c, krows=krows: k4[c, krows, :],
                           lambda c=c, krows=krows: v4[c, krows, :],
                           *bounds(j, nblk, sub), store_to(acc2, lse2, c, rows, slice(None))))
    run_blocks(blocks, sub, slope * 4)

    nsub = min(sub, n16)
    nk = nsub + 2 * half
    nblk = n16 // nsub
    blocks = []
    for c in range(4):
        for c16 in range(4):
            for j in range(nblk):
                rows_q = pl.ds(c16 + 4 * j * nsub, nsub, stride=4)
                rows_k = pl.ds(h4 + c16 + 4 * (j * nsub - half), nk, stride=4)
                blocks.append((lambda c=c, rows_q=rows_q: q4[c, rows_q, :],
                               lambda c=c, rows_k=rows_k: k4[c, rows_k, :],
                               lambda c=c, rows_k=rows_k: v4[c, rows_k, :],
                               *bounds(j, nblk, nsub),
                               store_to(acc3, lse3, c, rows_q, slice(None))))
    run_blocks(blocks, nsub, slope * 16)

    for c in range(4):
        rows = pl.ds(c, n4, stride=4)
        l1, l2, l3 = lse1[rows, :], lse2[c], lse3[c]
        mx = jnp.maximum(jnp.maximum(l1, l2), l3)
        e1, e2, e3 = jnp.exp(l1 - mx), jnp.exp(l2 - mx), jnp.exp(l3 - mx)
        o = (e1 * acc1[rows, :] + e2 * acc2[c] + e3 * acc3[c]) / (e1 + e2 + e3)
        on = o * lax.rsqrt(jnp.mean(o * o, axis=-1, keepdims=True) + RMS_EPS)
        ynat[rows, :] = on * g_ref[...]
    y_ref[0] = ynat[...].astype(y_ref.dtype)


def _dilated_attention(slopes, qkv, head_g):
    bsz, s, width3 = qkv.shape
    width = width3 // 3
    tq = min(ATTN_TILE, s)
    halo = ATTN_HALO
    assert tq % halo == 0 and s % tq == 0 and (tq // 16) % min(ATTN_SUB, tq // 16) == 0
    dh = ATTN_HEAD_DIM
    nh = width // dh
    nhb = tq // halo
    last_hb = s // halo - 1
    n4, h4 = tq // 4, halo // 4
    cur = lambda o: pl.BlockSpec((1, tq, dh), lambda b, t, c, sl: (b, t, o + c))
    prev = lambda o: pl.BlockSpec((1, halo, dh),
                                  lambda b, t, c, sl: (b, jnp.maximum(t * nhb - 1, 0), o + c))
    nxt = lambda o: pl.BlockSpec((1, halo, dh),
                                 lambda b, t, c, sl: (b, jnp.minimum((t + 1) * nhb, last_hb), o + c))
    kern = functools.partial(_attn_kernel, tq=tq)
    nstage = max(tq // ATTN_SUB, 16 * ((tq // 16) // min(ATTN_SUB, tq // 16)))
    win = pltpu.VMEM((4, n4 + 2 * h4, dh), F32)
    cls = pltpu.VMEM((4, n4, dh), F32)
    nat = pltpu.VMEM((tq, dh), F32)
    return pl.pallas_call(
        kern,
        grid_spec=pltpu.PrefetchScalarGridSpec(
            num_scalar_prefetch=1,
            grid=(bsz, s // tq, width // dh),
            in_specs=[cur(0), prev(nh), cur(nh), nxt(nh), prev(2 * nh), cur(2 * nh), nxt(2 * nh),
                      pl.BlockSpec((1, dh), lambda b, t, c, sl: (0, c))],
            out_specs=pl.BlockSpec((1, tq, dh), lambda b, t, c, sl: (b, t, c)),
            scratch_shapes=[cls, win, win, nat, nat, cls, cls, cls, cls, nat,
                            pltpu.VMEM((nstage, ATTN_SUB, ATTN_SUB + 2 * ATTN_HALF), F32),
                            pltpu.VMEM((nstage, ATTN_SUB, LANES), F32)]),
        out_shape=jax.ShapeDtypeStruct((bsz, s, width), BF16),
        compiler_params=_params(("parallel", "parallel", "parallel")),
        name="dilated_attn",
    )(slopes, qkv, qkv, qkv, qkv, qkv, qkv, qkv, head_g.reshape(1, width))


def _start_row_gather(idx_ref, src_hbm, dst, sem, nrows):
    def body(r, carry):
        pltpu.make_async_copy(src_hbm.at[pl.ds(idx_ref[0, 0, r], 1), :],
                              dst.at[pl.ds(r, 1), :], sem).start()
        return carry

    lax.fori_loop(0, nrows, body, 0, unroll=8)


def _wait_row_gather(src_hbm, dst, sem, nrows):
    pltpu.make_async_copy(src_hbm.at[pl.ds(0, nrows), :], dst, sem).wait()


def _round_weights(src_ref, dst_ref):
    rows_total = dst_ref.shape[0]
    step = min(WEIGHT_CAST_ROWS, rows_total)

    def body(c, carry):
        rows = pl.ds(pl.multiple_of(c * step, step), step)
        dst_ref[rows, :] = src_ref[rows, :].astype(dst_ref.dtype)
        return carry

    lax.fori_loop(0, rows_total // step, body, 0)


def _expert_changed(be_ref, i):
    return (i == 0) | (be_ref[i] != be_ref[jnp.maximum(i - 1, 0)])


def _expert_up_kernel(be_ref, nb_ref, idx_ref, idx_next_ref, h_hbm, wg_ref, wu_ref, o_ref,
                      xbuf, sem, wgb_ref, wub_ref):
    i = pl.program_id(0)
    n_used = nb_ref[0]
    slot = i % 2
    blk = xbuf.shape[1]

    @pl.when(i == 0)
    def _():
        _start_row_gather(idx_ref, h_hbm, xbuf.at[0], sem.at[0], blk)

    @pl.when(i + 1 < n_used)
    def _():
        _start_row_gather(idx_next_ref, h_hbm, xbuf.at[1 - slot], sem.at[1 - slot], blk)

    @pl.when((i < n_used) & _expert_changed(be_ref, i))
    def _():
        _round_weights(wg_ref, wgb_ref)
        _round_weights(wu_ref, wub_ref)

    @pl.when(i < n_used)
    def _():
        _wait_row_gather(h_hbm, xbuf.at[slot], sem.at[slot], blk)
        x = xbuf[slot].astype(BF16)
        g = jnp.dot(x, wgb_ref[...], preferred_element_type=F32)
        u = jnp.dot(x, wub_ref[...], preferred_element_type=F32)
        o_ref[...] = (g * jax.nn.sigmoid(g) * u).astype(o_ref.dtype)

    @pl.when(i >= n_used)
    def _():
        o_ref[...] = jnp.zeros_like(o_ref)


HI16 = 0xFFFF0000


def _pack_bf16_pairs(lo, hi):
    bits = lambda v: lax.bitcast_convert_type(v.astype(BF16).astype(F32), jnp.uint32)
    return (bits(hi) & jnp.uint32(HI16)) | (bits(lo) >> 16)


def _unpack_bf16_pairs(p):
    lo = lax.bitcast_convert_type(p << 16, F32)
    hi = lax.bitcast_convert_type(p & jnp.uint32(HI16), F32)
    return lo, hi


def _expert_down_kernel(be_ref, nb_ref, h_ref, wd_ref, o_ref, wdb_ref):
    i = pl.program_id(0)
    half = o_ref.shape[1]

    @pl.when((i < nb_ref[0]) & _expert_changed(be_ref, i))
    def _():
        _round_weights(wd_ref, wdb_ref)

    @pl.when(i < nb_ref[0])
    def _():
        y = jnp.dot(h_ref[...], wdb_ref[...], preferred_element_type=F32)
        o_ref[...] = _pack_bf16_pairs(y[:, :half], y[:, half:])

    @pl.when(i >= nb_ref[0])
    def _():
        o_ref[...] = jnp.zeros_like(o_ref)


def _expert_ffn(h, slot_tok, block_e, n_used, w_gate, w_up, w_down, layer):
    d = h.shape[1]
    p = slot_tok.shape[0]
    ff = w_gate.shape[3]
    blk = EXPERT_BLOCK
    nblk = p // blk
    idx = slot_tok.reshape(nblk, 1, blk)
    h1 = pl.pallas_call(
        _expert_up_kernel,
        grid_spec=pltpu.PrefetchScalarGridSpec(
            num_scalar_prefetch=2,
            grid=(nblk,),
            in_specs=[pl.BlockSpec((1, 1, blk), lambda i, be, nb: (i, 0, 0),
                                   memory_space=pltpu.SMEM),
                      pl.BlockSpec((1, 1, blk), lambda i, be, nb: (jnp.minimum(i + 1, nblk - 1), 0, 0),
                                   memory_space=pltpu.SMEM),
                      pl.BlockSpec(memory_space=pl.ANY),
                      pl.BlockSpec((None, None, d, ff), lambda i, be, nb: (layer, be[i], 0, 0),
                                   pipeline_mode=pl.Buffered(1)),
                      pl.BlockSpec((None, None, d, ff), lambda i, be, nb: (layer, be[i], 0, 0),
                                   pipeline_mode=pl.Buffered(1))],
            out_specs=pl.BlockSpec((blk, ff), lambda i, be, nb: (i, 0)),
            scratch_shapes=[pltpu.VMEM((2, blk, d), F32), pltpu.SemaphoreType.DMA((2,)),
                            pltpu.VMEM((d, ff), BF16), pltpu.VMEM((d, ff), BF16)]),
        out_shape=jax.ShapeDtypeStruct((p, ff), BF16),
        compiler_params=_params(("arbitrary",)),
        name="expert_up",
    )(block_e, n_used, idx, idx, h, w_gate, w_up)
    return pl.pallas_call(
        _expert_down_kernel,
        grid_spec=pltpu.PrefetchScalarGridSpec(
            num_scalar_prefetch=2,
            grid=(nblk,),
            in_specs=[pl.BlockSpec((blk, ff), lambda i, be, nb: (i, 0)),
                      pl.BlockSpec((None, None, ff, d), lambda i, be, nb: (layer, be[i], 0, 0))],
            out_specs=pl.BlockSpec((blk, d // 2), lambda i, be, nb: (i, 0)),
            scratch_shapes=[pltpu.VMEM((ff, d), BF16)]),
        out_shape=jax.ShapeDtypeStruct((p, d // 2), jnp.uint32),
        compiler_params=_params(("arbitrary",)),
        name="expert_down",
    )(block_e, n_used, h1, w_down)


def _combine_kernel(p0_ref, p1_ref, p0n_ref, p1n_ref, x_ref, w0_ref, w1_ref, g_ref, y_hbm,
                    *rest, emit_x):
    if emit_x:
        xo_ref, ho_ref, buf, sem = rest
    else:
        ho_ref, buf, sem = rest
    i = pl.program_id(0)
    nt = pl.num_programs(0)
    slot = i % 2
    tm = buf.shape[2]

    @pl.when(i == 0)
    def _():
        _start_row_gather(p0_ref, y_hbm, buf.at[0, 0], sem.at[0], tm)
        _start_row_gather(p1_ref, y_hbm, buf.at[0, 1], sem.at[0], tm)

    @pl.when(i + 1 < nt)
    def _():
        _start_row_gather(p0n_ref, y_hbm, buf.at[1 - slot, 0], sem.at[1 - slot], tm)
        _start_row_gather(p1n_ref, y_hbm, buf.at[1 - slot, 1], sem.at[1 - slot], tm)

    _wait_row_gather(y_hbm, buf.at[slot, 0], sem.at[slot], tm)
    _wait_row_gather(y_hbm, buf.at[slot, 1], sem.at[slot], tm)
    half = buf.shape[3]
    lo0, hi0 = _unpack_bf16_pairs(buf[slot, 0])
    lo1, hi1 = _unpack_bf16_pairs(buf[slot, 1])
    w0, w1 = w0_ref[...], w1_ref[...]
    x_lo = x_ref[:, :half] + w0 * lo0 + w1 * lo1
    x_hi = x_ref[:, half:] + w0 * hi0 + w1 * hi1
    if emit_x:
        xo_ref[:, :half] = x_lo
        xo_ref[:, half:] = x_hi
    ssq = (jnp.sum(x_lo * x_lo, axis=-1, keepdims=True)
           + jnp.sum(x_hi * x_hi, axis=-1, keepdims=True))
    inv = lax.rsqrt(ssq * (1.0 / (2 * half)) + RMS_EPS)
    ho_ref[:, :half] = (x_lo * inv * g_ref[:, :half]).astype(ho_ref.dtype)
    ho_ref[:, half:] = (x_hi * inv * g_ref[:, half:]).astype(ho_ref.dtype)


def _combine_norm(x2d, y, pos, wts, g, emit_x, tm=256):
    t, d = x2d.shape
    nt = t // tm
    pidx = pos.reshape(EXPERT_TOPK, nt, 1, tm)
    wcol = wts.reshape(EXPERT_TOPK, t, 1)
    cur = lambda i: (i, 0, 0)
    nxt = lambda i: (jnp.minimum(i + 1, nt - 1), 0, 0)
    ispec = lambda f: pl.BlockSpec((1, 1, tm), f, memory_space=pltpu.SMEM)
    row = pl.BlockSpec((tm, d), lambda i: (i, 0))
    wspec = pl.BlockSpec((tm, 1), lambda i: (i, 0))
    if emit_x:
        out_specs = [row, row]
        out_shape = [jax.ShapeDtypeStruct((t, d), F32), jax.ShapeDtypeStruct((t, d), BF16)]
    else:
        out_specs = row
        out_shape = jax.ShapeDtypeStruct((t, d), F32)
    return pl.pallas_call(
        functools.partial(_combine_kernel, emit_x=emit_x),
        grid=(nt,),
        in_specs=[ispec(cur), ispec(cur), ispec(nxt), ispec(nxt), row, wspec, wspec,
                  pl.BlockSpec((1, d), lambda i: (0, 0)),
                  pl.BlockSpec(memory_space=pl.ANY)],
        out_specs=out_specs,
        out_shape=out_shape,
        scratch_shapes=[pltpu.VMEM((2, EXPERT_TOPK, tm, d // 2), jnp.uint32),
                        pltpu.SemaphoreType.DMA((2,))],
        compiler_params=_params(("arbitrary",)),
        name="moe_combine_norm",
    )(pidx[0], pidx[1], pidx[0], pidx[1], x2d, wcol[0], wcol[1], g.reshape(1, d), y)


def _route(eid, n_experts):
    t = eid.shape[1]
    a = EXPERT_TOPK * t
    flat_e = eid.reshape(a)
    onehot = (flat_e[:, None] == jnp.arange(n_experts, dtype=jnp.int32)[None, :]).astype(jnp.int32)
    csum = jnp.cumsum(onehot, axis=0)
    counts = csum[-1]
    blocks = (counts + EXPERT_BLOCK - 1) // EXPERT_BLOCK
    blk_end = jnp.cumsum(blocks)
    pad_start = (blk_end - blocks) * EXPERT_BLOCK
    dest = jnp.sum(onehot * (csum - 1 + pad_start[None, :]), axis=1)
    n_blocks = -(-a // EXPERT_BLOCK) + n_experts
    p = n_blocks * EXPERT_BLOCK
    tok = jnp.arange(a, dtype=jnp.int32) % t
    slot_tok = jnp.zeros((p,), jnp.int32).at[dest].set(tok)
    block_e = jnp.sum((blk_end[None, :] <= jnp.arange(n_blocks, dtype=jnp.int32)[:, None])
                      .astype(jnp.int32), axis=1)
    block_e = jnp.minimum(block_e, n_experts - 1)
    n_used = blk_end[-1:].astype(jnp.int32)
    return slot_tok, block_e, n_used, dest.reshape(EXPERT_TOPK, t)


def kernel(x, norm1_g, w_in, b_gate, conv_w, conv_b, head_norm_g, w_out, norm2_g, w_router_group,
           b_router_group, w_router_expert, b_router_expert, w_gate, w_up, w_down, final_norm_g):
    depth = norm1_g.shape[0]
    bsz, s, d = x.shape
    t = bsz * s
    mw = d // 2
    nmh = mw // MLSTM_DV
    qkw = nmh * MLSTM_DK
    aw = d - mw
    nah = aw // ATTN_HEAD_DIM
    ngate = 4 * nmh
    n_groups = w_router_group.shape[2]
    n_experts = w_router_expert.shape[2]

    c_v = 2 * qkw
    c_o = c_v + mw
    c_g = c_o + mw
    c_a = c_g + ngate
    assert ngate <= LANES and c_a + 3 * aw == w_in.shape[2]
    w_in_t = jnp.swapaxes(w_in, 1, 2)
    npad = ROUTER_ROWS - n_groups - n_experts
    w_router_t = jnp.pad(jnp.concatenate([w_router_group, w_router_expert], axis=2)
                         .transpose(0, 2, 1), ((0, 0), (0, npad), (0, 0)))
    b_router_t = jnp.pad(jnp.concatenate([b_router_group, b_router_expert], axis=1),
                         ((0, 0), (0, npad))).reshape(depth, ROUTER_ROWS, 1)
    slopes = jnp.exp2(-8.0 * jnp.arange(1, nah + 1, dtype=F32) / nah)

    x2d = x.reshape(t, d)
    h = _rmsnorm(x2d, norm1_g[0], BF16)
    for l in range(depth):
        head_g = head_norm_g[l]
        z_m = _matmul(h, w_in_t, l, F32, row0=0, n=c_g).reshape(bsz, s, c_g)
        gates = _matmul(h, w_in_t, l, F32, row0=c_g, n=LANES)[:, :ngate]
        z_a = _matmul(h, w_in_t, l, F32, row0=c_a, n=3 * aw).reshape(bsz, s, 3 * aw)

        q_m = _short_conv_silu(z_m, conv_w[l], conv_b[l], 0, qkw, MLSTM_DK ** -0.5,
                               transpose=False)
        kt_m = _short_conv_silu(z_m, conv_w[l], conv_b[l], qkw, qkw, 1.0, transpose=True)
        gates_t = gates.reshape(bsz, s, 4, nmh).transpose(0, 2, 3, 1).reshape(bsz, 4, nmh, 1, s)
        h_bwd = _mlstm_scan(b_gate[l], q_m, kt_m, z_m, 1, gates_t, reverse=True)
        y_m = _mlstm_scan(b_gate[l], q_m, kt_m, z_m, 1, gates_t, reverse=False, hprev=h_bwd,
                          o_blk=2, head_g=head_g[:mw].reshape(1, mw))

        y_a = _dilated_attention(slopes, z_a, head_g[mw:])

        x2d = _out_proj(y_m.reshape(t, mw), y_a.reshape(t, aw), w_out, l, x2d)

        h2, eid, wts = _rmsnorm_router(x2d, norm2_g[l], w_router_t[l], b_router_t[l],
                                       n_groups, n_experts // n_groups)
        slot_tok, block_e, n_used, pos = _route(eid[:EXPERT_TOPK], n_experts)
        yb = _expert_ffn(h2, slot_tok, block_e, n_used, w_gate, w_up, w_down, l)
        if l + 1 < depth:
            x2d, h = _combine_norm(x2d, yb, pos, wts[:EXPERT_TOPK], norm1_g[l + 1], emit_x=True)
        else:
            out = _combine_norm(x2d, yb, pos, wts[:EXPERT_TOPK], final_norm_g, emit_x=False)
    return out.reshape(bsz, s, d)
```
